```python
import jax, jax.numpy as jnp
from jax import lax
import numpy as np

D_MODEL = 1024
BATCH = 4
SEQ = 8192
DEPTH = 1
DEC_BATCH = 8
DEC_SEQ = 16
PAST_LEN = 1024

CHUNK = 64
GM_CHUNK = 128
GM_GROUPS = 8
GM_WIDTH = 1024
GM_GROUP_DIM = GM_WIDTH // GM_GROUPS
N_HEADS = 16
N_KV_HEADS = 2
HEAD_DIM = 64
Q_PER_KV = N_HEADS // N_KV_HEADS
WINDOW = 128
WINDOW_CHUNKS = WINDOW // CHUNK
BAND = (WINDOW_CHUNKS + 1) * CHUNK
QW = N_HEADS * HEAD_DIM
KVW = N_KV_HEADS * HEAD_DIM
D_FF = 2816
EPS = 1e-6
NEG = -1e30
SPLIT_POINTS = (GM_WIDTH, 2 * GM_WIDTH, 2 * GM_WIDTH + QW, 2 * GM_WIDTH + QW + KVW,
                2 * GM_WIDTH + QW + 2 * KVW, 2 * GM_WIDTH + QW + 2 * KVW + D_MODEL)
IN_COLS = 2 * GM_WIDTH + QW + 2 * KVW + 2 * D_MODEL

kernel_name = "hybrid_gmlp_swa_sink_streaming_step"


def _rmsnorm(x, g):
    xf = x.astype(jnp.float32)
    y = xf * lax.rsqrt(jnp.mean(xf * xf, axis=-1, keepdims=True) + EPS)
    return (y * g.astype(jnp.float32)).astype(x.dtype)


def _swiglu_half(x, g, w1, w3, w2):
    h = _rmsnorm(x, g)
    return x + 0.5 * ((jax.nn.silu(h @ w1) * (h @ w3)) @ w2)


def _mixer_inputs(x, g, w_in, gm_norm):
    h = _rmsnorm(x, g)
    u, v, q, k, va, ga, gb = jnp.split(h @ w_in, SPLIT_POINTS, axis=-1)
    u = jax.nn.gelu(u)
    v_n = _rmsnorm(jax.nn.gelu(v), gm_norm)
    return u, v_n, q, k, va, ga, gb


def _gm_mask():
    i = jnp.arange(GM_CHUNK)
    return (i[:, None] // CHUNK) >= (i[None, :] // CHUNK)


def _sink_attention(q, k, v, mask, sinks):
    s = jnp.einsum('bnqkgd,bnskd->bnkgqs', q, k).astype(jnp.float32) * (HEAD_DIM ** -0.5)
    s = jnp.where(mask[None, :, None, None], s, NEG)
    snk = jnp.broadcast_to(
        sinks.astype(jnp.float32).reshape(N_KV_HEADS, Q_PER_KV)[:, :, None, None],
        s.shape[:-1] + (1,))
    p = jax.nn.softmax(jnp.concatenate([s, snk], axis=-1), axis=-1)[..., :-1]
    return jnp.einsum('bnkgqs,bnskd->bnqkgd', p.astype(v.dtype), v)


def _band(t, nc):
    b = t.shape[0]
    tp = jnp.pad(t, ((0, 0), (WINDOW, 0), (0, 0), (0, 0)))
    tp = tp.reshape(b, nc + WINDOW_CHUNKS, CHUNK, N_KV_HEADS, HEAD_DIM)
    return jnp.concatenate([tp[:, i:i + nc] for i in range(WINDOW_CHUNKS + 1)], axis=2)


def _mix_prompt(u, v_n, q, k, va, ws, bs, sinks):
    b, t, _ = u.shape
    ws_m = ws * _gm_mask().astype(ws.dtype)
    vr = v_n.reshape(b, t // GM_CHUNK, GM_CHUNK, GM_GROUPS, GM_GROUP_DIM)
    sg = jnp.einsum('gij,bcjgd->bcigd', ws_m, vr) + bs.T[:, :, None]
    o_a = u * sg.reshape(b, t, GM_WIDTH)
    nc = t // CHUNK
    qc = q.reshape(b, nc, CHUNK, N_KV_HEADS, Q_PER_KV, HEAD_DIM)
    k4 = k.reshape(b, t, N_KV_HEADS, HEAD_DIM)
    v4 = va.reshape(b, t, N_KV_HEADS, HEAD_DIM)
    key_pos = jnp.arange(nc)[:, None] * CHUNK - WINDOW + jnp.arange(BAND)[None, :]
    mask = (key_pos >= 0)[:, None, :]
    o_b = _sink_attention(qc, _band(k4, nc), _band(v4, nc), mask, sinks).reshape(b, t, QW)
    keep = min(WINDOW, t)
    return o_a, o_b, k4[:, t - keep:], v4[:, t - keep:]


def _mix_sample(u, v_n, q, k, va, ck, cv, ws, bs, sinks):
    b, s, _ = u.shape
    wc = ck.shape[1]
    ws_m = (ws * _gm_mask().astype(ws.dtype))[:, :s, :s]
    vr = v_n.reshape(b, s, GM_GROUPS, GM_GROUP_DIM)
    sg = jnp.einsum('gij,bjgd->bigd', ws_m, vr) + bs[:, :s].T[:, :, None]
    o_a = u * sg.reshape(b, s, GM_WIDTH)
    k4 = k.reshape(b, s, N_KV_HEADS, HEAD_DIM)
    v4 = va.reshape(b, s, N_KV_HEADS, HEAD_DIM)
    q_pos = PAST_LEN + jnp.arange(s)
    k_pos = jnp.concatenate([PAST_LEN - wc + jnp.arange(wc), q_pos])
    dist = q_pos[:, None] // CHUNK - k_pos[None, :] // CHUNK
    mask = ((dist >= 0) & (dist <= WINDOW_CHUNKS))[None]
    keys = jnp.concatenate([ck, k4], axis=1)[:, None]
    vals = jnp.concatenate([cv, v4], axis=1)[:, None]
    qs = q.reshape(b, 1, s, N_KV_HEADS, Q_PER_KV, HEAD_DIM)
    o_b = _sink_attention(qs, keys, vals, mask, sinks).reshape(b, s, QW)
    return o_a, o_b, k4, v4, v_n


def _merge(x, o_a, o_b, ga, gb, w_pa, w_pb, w_out):
    m = jax.nn.sigmoid(ga) * (o_a @ w_pa) + jax.nn.sigmoid(gb) * (o_b @ w_pb)
    return x + m @ w_out


def setup_inputs(seed: int = 0) -> dict:
    key = jax.random.key(seed)
    ks = jax.random.split(key, 24)
    f32 = jnp.float32

    def nrm(k, shape, scale):
        return jax.random.normal(k, shape, f32) * scale

    wc = min(WINDOW, PAST_LEN)
    return {
        "x_prompt": nrm(ks[0], (BATCH, SEQ, D_MODEL), 1.0),
        "x_sample": nrm(ks[1], (DEC_BATCH, DEC_SEQ, D_MODEL), 1.0),
        "cache_k": nrm(ks[2], (DEPTH, DEC_BATCH, wc, N_KV_HEADS, HEAD_DIM), 1.0),
        "cache_v": nrm(ks[3], (DEPTH, DEC_BATCH, wc, N_KV_HEADS, HEAD_DIM), 1.0),
        "norm_ffn1": 1.0 + nrm(ks[4], (DEPTH, D_MODEL), 0.02),
        "ffn1_w1": nrm(ks[5], (DEPTH, D_MODEL, D_FF), D_MODEL ** -0.5),
        "ffn1_w3": nrm(ks[6], (DEPTH, D_MODEL, D_FF), D_MODEL ** -0.5),
        "ffn1_w2": nrm(ks[7], (DEPTH, D_FF, D_MODEL), D_FF ** -0.5),
        "norm_mix": 1.0 + nrm(ks[8], (DEPTH, D_MODEL), 0.02),
        "w_in": nrm(ks[9], (DEPTH, D_MODEL, IN_COLS), D_MODEL ** -0.5),
        "gm_norm": 1.0 + nrm(ks[10], (DEPTH, GM_WIDTH), 0.02),
        "gm_ws": nrm(ks[11], (DEPTH, GM_GROUPS, GM_CHUNK, GM_CHUNK), GM_CHUNK ** -0.5),
        "gm_bs": 1.0 + nrm(ks[12], (DEPTH, GM_GROUPS, GM_CHUNK), 0.02),
        "sinks": nrm(ks[13], (DEPTH, N_HEADS), 0.5),
        "w_pa": nrm(ks[14], (DEPTH, GM_WIDTH, D_MODEL), GM_WIDTH ** -0.5),
        "w_pb": nrm(ks[15], (DEPTH, QW, D_MODEL), QW ** -0.5),
        "w_out": nrm(ks[16], (DEPTH, D_MODEL, D_MODEL), D_MODEL ** -0.5),
        "norm_ffn2": 1.0 + nrm(ks[17], (DEPTH, D_MODEL), 0.02),
        "ffn2_w1": nrm(ks[18], (DEPTH, D_MODEL, D_FF), D_MODEL ** -0.5),
        "ffn2_w3": nrm(ks[19], (DEPTH, D_MODEL, D_FF), D_MODEL ** -0.5),
        "ffn2_w2": nrm(ks[20], (DEPTH, D_FF, D_MODEL), D_FF ** -0.5),
        "norm_final": 1.0 + nrm(ks[21], (D_MODEL,), 0.02),
    }


def reference(x_prompt, x_sample, cache_k, cache_v, norm_ffn1, ffn1_w1, ffn1_w3, ffn1_w2,
              norm_mix, w_in, gm_norm, gm_ws, gm_bs, sinks, w_pa, w_pb, w_out,
              norm_ffn2, ffn2_w1, ffn2_w3, ffn2_w2, norm_final):
    xp, xs = x_prompt, x_sample
    kp_l, vp_l, ks_l, vs_l, gs_l = [], [], [], [], []
    for l in range(DEPTH):
        xp = _swiglu_half(xp, norm_ffn1[l], ffn1_w1[l], ffn1_w3[l], ffn1_w2[l])
        xs = _swiglu_half(xs, norm_ffn1[l], ffn1_w1[l], ffn1_w3[l], ffn1_w2[l])
        u, v_n, q, k, va, ga, gb = _mixer_inputs(xp, norm_mix[l], w_in[l], gm_norm[l])
        o_a, o_b, nk, nv = _mix_prompt(u, v_n, q, k, va, gm_ws[l], gm_bs[l], sinks[l])
        xp = _merge(xp, o_a, o_b, ga, gb, w_pa[l], w_pb[l], w_out[l])
        kp_l.append(nk)
        vp_l.append(nv)
        u, v_n, q, k, va, ga, gb = _mixer_inputs(xs, norm_mix[l], w_in[l], gm_norm[l])
        o_a, o_b, nk, nv, gv = _mix_sample(u, v_n, q, k, va, cache_k[l], cache_v[l],
                                           gm_ws[l], gm_bs[l], sinks[l])
        xs = _merge(xs, o_a, o_b, ga, gb, w_pa[l], w_pb[l], w_out[l])
        ks_l.append(nk)
        vs_l.append(nv)
        gs_l.append(gv)
        xp = _swiglu_half(xp, norm_ffn2[l], ffn2_w1[l], ffn2_w3[l], ffn2_w2[l])
        xs = _swiglu_half(xs, norm_ffn2[l], ffn2_w1[l], ffn2_w3[l], ffn2_w2[l])
    y_prompt = _rmsnorm(xp, norm_final)
    y_sample = _rmsnorm(xs, norm_final)
    return (y_prompt, y_sample, jnp.stack(kp_l), jnp.stack(vp_l),
            jnp.stack(ks_l), jnp.stack(vs_l), jnp.stack(gs_l))
```

```python
import functools

import jax
import jax.numpy as jnp
from jax import lax
from jax.experimental import pallas as pl
from jax.experimental.pallas import tpu as pltpu

D_MODEL = 1024
PAST_LEN = 1024
CHUNK = 64
GM_CHUNK = 128
GM_GROUPS = 8
GM_WIDTH = 1024
GM_GROUP_DIM = GM_WIDTH // GM_GROUPS
N_HEADS = 16
N_KV_HEADS = 2
HEAD_DIM = 64
Q_PER_KV = N_HEADS // N_KV_HEADS
WINDOW = 128
WINDOW_CHUNKS = WINDOW // CHUNK
BAND = WINDOW + CHUNK
QW = N_HEADS * HEAD_DIM
KVW = N_KV_HEADS * HEAD_DIM
D_FF = 2816
EPS = 1e-6
NEG = -1e30
SCALE = HEAD_DIM ** -0.5

C_U = 0
C_V = GM_WIDTH
C_Q = 2 * GM_WIDTH
C_KV = C_Q + QW
C_GA = C_KV + 2 * KVW
C_GB = C_GA + D_MODEL
IN_COLS = C_GB + D_MODEL

LANES = 128
SLAB_HEADS = LANES // HEAD_DIM
KV_SLABS = Q_PER_KV // SLAB_HEADS
FF_TILE = 256
VMEM_LIMIT = 56 * 1024 * 1024

F32 = jnp.float32
BF16 = jnp.bfloat16


def _rms(x, g):
    return x * lax.rsqrt(jnp.mean(x * x, axis=-1, keepdims=True) + EPS) * g


def _dot(a, b):
    return jnp.dot(a, b, preferred_element_type=F32)


def _dot_nt(a, b):
    return lax.dot_general(a, b, (((1,), (1,)), ((), ())), preferred_element_type=F32)


def _const_spec(shape):
    zeros = (0,) * len(shape)
    return pl.BlockSpec(shape, lambda *_: zeros, pipeline_mode=pl.Buffered(1))


def _params(n_axes):
    return pltpu.CompilerParams(dimension_semantics=("arbitrary",) * n_axes,
                                vmem_limit_bytes=VMEM_LIMIT)


def _ffn_body(x_ref, g_ref, w1_ref, w3_ref, w2_ref, *rest, final):
    if final:
        gf_ref, o_ref = rest
    else:
        (o_ref,) = rest
    x = x_ref[...]
    h = _rms(x, g_ref[...]).astype(BF16)
    acc = jnp.zeros(x.shape, F32)
    for f in range(D_FF // FF_TILE):
        sl = slice(f * FF_TILE, (f + 1) * FF_TILE)
        a = _dot(h, w1_ref[:, sl])
        b = _dot(h, w3_ref[:, sl])
        gated = (a * jax.nn.sigmoid(a) * b).astype(BF16)
        acc = acc + _dot(gated, w2_ref[sl, :])
    y = x + 0.5 * acc
    if final:
        y = _rms(y, gf_ref[...])
    o_ref[...] = y


def _ffn(x, g, w1, w3, w2, gf, tm):
    n = x.shape[0]
    final = gf is not None
    row = pl.BlockSpec((tm, D_MODEL), lambda i: (i, 0))
    in_specs = [row, _const_spec((1, D_MODEL)), _const_spec((D_MODEL, D_FF)),
                _const_spec((D_MODEL, D_FF)), _const_spec((D_FF, D_MODEL))]
    args = [x, g, w1, w3, w2]
    if final:
        in_specs.append(_const_spec((1, D_MODEL)))
        args.append(gf)
    return pl.pallas_call(
        functools.partial(_ffn_body, final=final),
        grid=(n // tm,),
        in_specs=in_specs,
        out_specs=row,
        out_shape=jax.ShapeDtypeStruct((n, D_MODEL), F32),
        compiler_params=_params(1),
        name="ffn_final" if final else "ffn",
    )(*args)


def _inproj_body(x_ref, g_ref, w_ref, gm_ref, u_ref, v_ref, q_ref, kv_ref, ga_ref, gb_ref):
    h = _rms(x_ref[...], g_ref[...]).astype(BF16)

    def proj(lo, hi):
        return _dot(h, w_ref[:, lo:hi])

    u_ref[...] = jax.nn.gelu(proj(C_U, C_V)).astype(u_ref.dtype)
    v_ref[...] = _rms(jax.nn.gelu(proj(C_V, C_Q)), gm_ref[...]).astype(v_ref.dtype)
    q_ref[...] = (proj(C_Q, C_KV) * SCALE).astype(q_ref.dtype)
    kv_ref[...] = proj(C_KV, C_GA)
    ga_ref[...] = jax.nn.sigmoid(proj(C_GA, C_GB)).astype(ga_ref.dtype)
    gb_ref[...] = jax.nn.sigmoid(proj(C_GB, IN_COLS)).astype(gb_ref.dtype)


def _inproj(x, g, w_in, gm, tm, v_dtype):
    n = x.shape[0]
    row = lambda w: pl.BlockSpec((tm, w), lambda i: (i, 0))
    wide = jax.ShapeDtypeStruct((n, D_MODEL), BF16)
    return pl.pallas_call(
        _inproj_body,
        grid=(n // tm,),
        in_specs=[row(D_MODEL), _const_spec((1, D_MODEL)), _const_spec((D_MODEL, IN_COLS)),
                  _const_spec((1, GM_WIDTH))],
        out_specs=[row(GM_WIDTH), row(GM_WIDTH), row(QW), row(2 * KVW), row(D_MODEL), row(D_MODEL)],
        out_shape=[wide, jax.ShapeDtypeStruct((n, GM_WIDTH), v_dtype), wide,
                   jax.ShapeDtypeStruct((n, 2 * KVW), F32), wide, wide],
        compiler_params=_params(1),
        name="inproj",
    )(x, g, w_in, gm)


def _stage_kv(kv, kst_ref, vst_ref):
    k2 = kv[:, :KVW]
    v2 = kv[:, KVW:]
    low = lax.broadcasted_iota(jnp.int32, k2.shape, 1) < HEAD_DIM
    for src, dst in ((k2, kst_ref), (v2, vst_ref)):
        rolled = pltpu.roll(src, HEAD_DIM, 1)
        dst[0] = jnp.where(low, src, 0.0).astype(BF16)
        dst[1] = jnp.where(low, 0.0, rolled).astype(BF16)
        dst[2] = jnp.where(low, rolled, 0.0).astype(BF16)
        dst[3] = jnp.where(low, 0.0, src).astype(BF16)


def _sink_columns(sinks_ref, rows_per_slab):
    n = KV_SLABS * rows_per_slab
    row = lax.broadcasted_iota(jnp.int32, (n, 1), 0)
    cols = []
    for kh in range(N_KV_HEADS):
        for j in range(SLAB_HEADS):
            head = lambda p: kh * Q_PER_KV + SLAB_HEADS * p + j
            col = jnp.full((n, 1), sinks_ref[head(KV_SLABS - 1)], F32)
            for p in range(KV_SLABS - 2, -1, -1):
                col = jnp.where(row < (p + 1) * rows_per_slab, sinks_ref[head(p)], col)
            cols.append(col)
    return cols


def _attend(lhs, keys, vals, valid, sink_cols):
    out = None
    for j in range(SLAB_HEADS):
        s = _dot_nt(lhs, keys[j])
        s = jnp.where(valid, s, NEG)
        snk = sink_cols[j]
        m = jnp.maximum(jnp.max(s, axis=1, keepdims=True), snk)
        e = jnp.exp(s - m)
        den = jnp.sum(e, axis=1, keepdims=True) + jnp.exp(snk - m)
        o = _dot((e / den).astype(BF16), vals[j])
        out = o if out is None else out + o
    return out


def _merge(x, oa, ob, ga, gb, wpa_ref, wpb_ref, wout_ref):
    ya = _dot(oa, wpa_ref[...])
    yb = _dot(ob, wpb_ref[...])
    m = (ga.astype(F32) * ya + gb.astype(F32) * yb).astype(BF16)
    return x + _dot(m, wout_ref[...])


def _div(x, d):
    assert d & (d - 1) == 0
    return lax.shift_right_logical(x, d.bit_length() - 1)


def _mod(x, d):
    assert d & (d - 1) == 0
    return x & (d - 1)


def _chunk_causal(i, j):
    return _div(i, CHUNK) >= _div(j, CHUNK)


def _mix_prompt_body(sinks_ref, x_ref, u_ref, v_ref, q_ref, kvp_ref, kvc_ref, ga_ref, gb_ref,
                     ws_ref, bst_ref, wpa_ref, wpb_ref, wout_ref, o_ref,
                     kst_ref, vst_ref, oa_ref, ob_ref, *, tm):
    i = pl.program_id(1)

    n_gm = tm // GM_CHUNK
    ri = lax.broadcasted_iota(jnp.int32, (GM_CHUNK, GM_CHUNK), 0)
    ci = lax.broadcasted_iota(jnp.int32, (GM_CHUNK, GM_CHUNK), 1)
    gm_mask = _chunk_causal(ri, ci).astype(F32)
    for g in range(GM_GROUPS):
        cols = slice(g * GM_GROUP_DIM, (g + 1) * GM_GROUP_DIM)
        w = (ws_ref[g] * gm_mask).astype(BF16)
        rhs = jnp.concatenate(
            [v_ref[0, c * GM_CHUNK:(c + 1) * GM_CHUNK, cols] for c in range(n_gm)], axis=1)
        sg = _dot(w, rhs) + bst_ref[:, g:g + 1]
        for c in range(n_gm):
            rows = slice(c * GM_CHUNK, (c + 1) * GM_CHUNK)
            oa_ref[rows, cols] = (u_ref[0, rows, cols].astype(F32)
                                  * sg[:, c * GM_CHUNK:(c + 1) * GM_CHUNK]).astype(BF16)

    _stage_kv(jnp.concatenate([kvp_ref[0], kvc_ref[0]], axis=0), kst_ref, vst_ref)
    sink_cols = _sink_columns(sinks_ref, CHUNK)

    def chunk(c, carry):
        r0 = pl.multiple_of(c * CHUNK, CHUNK)
        key_pos = (i * tm + c * CHUNK - WINDOW
                   + lax.broadcasted_iota(jnp.int32, (KV_SLABS * CHUNK, BAND), 1))
        valid = key_pos >= 0
        for kh in range(N_KV_HEADS):
            slab = lambda p: slice((kh * KV_SLABS + p) * LANES, (kh * KV_SLABS + p + 1) * LANES)
            lhs = jnp.concatenate(
                [q_ref[0, pl.ds(r0, CHUNK), slab(p)] for p in range(KV_SLABS)], axis=0)
            keys = [kst_ref[kh * SLAB_HEADS + j, pl.ds(r0, BAND), :] for j in range(SLAB_HEADS)]
            vals = [vst_ref[kh * SLAB_HEADS + j, pl.ds(r0, BAND), :] for j in range(SLAB_HEADS)]
            o = _attend(lhs, keys, vals, valid,
                        sink_cols[kh * SLAB_HEADS:(kh + 1) * SLAB_HEADS])
            for p in range(KV_SLABS):
                ob_ref[pl.ds(r0, CHUNK), slab(p)] = o[p * CHUNK:(p + 1) * CHUNK].astype(BF16)
        return carry

    lax.fori_loop(0, tm // CHUNK, chunk, 0)

    o_ref[0] = _merge(x_ref[0], oa_ref[...], ob_ref[...], ga_ref[0], gb_ref[0],
                      wpa_ref, wpb_ref, wout_ref)


def _mix_prompt(sinks, x, u, v, q, kv, ga, gb, ws, bst, wpa, wpb, wout, tm):
    b, t, _ = x.shape
    blk = lambda w: pl.BlockSpec((1, tm, w), lambda bi, i, s: (bi, i, 0))
    prev = pl.BlockSpec((1, WINDOW, 2 * KVW),
                        lambda bi, i, s: (bi, jnp.maximum(i * (tm // WINDOW) - 1, 0), 0))
    grid_spec = pltpu.PrefetchScalarGridSpec(
        num_scalar_prefetch=1,
        grid=(b, t // tm),
        in_specs=[blk(D_MODEL), blk(GM_WIDTH), blk(GM_WIDTH), blk(QW), prev, blk(2 * KVW),
                  blk(D_MODEL), blk(D_MODEL),
                  _const_spec((GM_GROUPS, GM_CHUNK, GM_CHUNK)), _const_spec((GM_CHUNK, GM_GROUPS)),
                  _const_spec((GM_WIDTH, D_MODEL)), _const_spec((QW, D_MODEL)),
                  _const_spec((D_MODEL, D_MODEL))],
        out_specs=blk(D_MODEL),
        scratch_shapes=[pltpu.VMEM((N_KV_HEADS * SLAB_HEADS, tm + WINDOW, LANES), BF16),
                        pltpu.VMEM((N_KV_HEADS * SLAB_HEADS, tm + WINDOW, LANES), BF16),
                        pltpu.VMEM((tm, GM_WIDTH), BF16),
                        pltpu.VMEM((tm, QW), BF16)],
    )
    return pl.pallas_call(
        functools.partial(_mix_prompt_body, tm=tm),
        grid_spec=grid_spec,
        out_shape=jax.ShapeDtypeStruct((b, t, D_MODEL), F32),
        compiler_params=_params(2),
        name="mix_prompt",
    )(sinks, x, u, v, q, kv, kv, ga, gb, ws, bst, wpa, wpb, wout)


def _mix_sample_body(sinks_ref, x_ref, u_ref, v_ref, q_ref, kv_ref, ck_ref, cv_ref, ga_ref, gb_ref,
                     wst_ref, bst_ref, wpa_ref, wpb_ref, wout_ref, o_ref,
                     kst_ref, vst_ref, oa_ref, ob_ref, *, nb, s, wc):
    n = nb * s
    ri = lax.broadcasted_iota(jnp.int32, (n, n), 0)
    ci = lax.broadcasted_iota(jnp.int32, (n, n), 1)
    gm_mask = ((_div(ri, s) == _div(ci, s)) & _chunk_causal(_mod(ri, s), _mod(ci, s))).astype(F32)
    for g in range(GM_GROUPS):
        cols = slice(g * GM_GROUP_DIM, (g + 1) * GM_GROUP_DIM)
        w = (wst_ref[g] * gm_mask).astype(BF16)
        sg = _dot(w, v_ref[:, cols].astype(BF16)) + bst_ref[:, g:g + 1]
        oa_ref[:, cols] = (u_ref[:, cols].astype(F32) * sg).astype(BF16)

    sink_cols = _sink_columns(sinks_ref, s)
    rows = KV_SLABS * s
    q_pos = PAST_LEN + _mod(lax.broadcasted_iota(jnp.int32, (rows, wc + s), 0), s)
    k_pos = PAST_LEN - wc + lax.broadcasted_iota(jnp.int32, (rows, wc + s), 1)
    dist = _div(q_pos, CHUNK) - _div(k_pos, CHUNK)
    valid = (dist >= 0) & (dist <= WINDOW_CHUNKS)
    for b in range(nb):
        r = slice(b * s, (b + 1) * s)
        new = kv_ref[r, :]
        kv = jnp.concatenate(
            [jnp.concatenate([ck_ref[b], new[:, :KVW]], axis=0),
             jnp.concatenate([cv_ref[b], new[:, KVW:]], axis=0)], axis=1)
        _stage_kv(kv, kst_ref, vst_ref)
        for kh in range(N_KV_HEADS):
            slab = lambda p: slice((kh * KV_SLABS + p) * LANES, (kh * KV_SLABS + p + 1) * LANES)
            lhs = jnp.concatenate([q_ref[r, slab(p)] for p in range(KV_SLABS)], axis=0).astype(BF16)
            keys = [kst_ref[kh * SLAB_HEADS + j] for j in range(SLAB_HEADS)]
            vals = [vst_ref[kh * SLAB_HEADS + j] for j in range(SLAB_HEADS)]
            o = _attend(lhs, keys, vals, valid, sink_cols[kh * SLAB_HEADS:(kh + 1) * SLAB_HEADS])
            for p in range(KV_SLABS):
                ob_ref[r, slab(p)] = o[p * s:(p + 1) * s].astype(BF16)

    o_ref[...] = _merge(x_ref[...], oa_ref[...], ob_ref[...], ga_ref[...], gb_ref[...],
                        wpa_ref, wpb_ref, wout_ref)


def _mix_sample(sinks, x, u, v, q, kv, ck, cv, ga, gb, wst, bst, wpa, wpb, wout, nb, s):
    n = nb * s
    wc = ck.shape[1]
    full = lambda a: pl.BlockSpec(a.shape, lambda i, sk, nd=a.ndim: (0,) * nd)
    args = (x, u, v, q, kv, ck, cv, ga, gb, wst, bst, wpa, wpb, wout)
    grid_spec = pltpu.PrefetchScalarGridSpec(
        num_scalar_prefetch=1,
        grid=(1,),
        in_specs=[full(a) for a in args],
        out_specs=pl.BlockSpec((n, D_MODEL), lambda i, sk: (0, 0)),
        scratch_shapes=[pltpu.VMEM((N_KV_HEADS * SLAB_HEADS, wc + s, LANES), BF16),
                        pltpu.VMEM((N_KV_HEADS * SLAB_HEADS, wc + s, LANES), BF16),
                        pltpu.VMEM((n, GM_WIDTH), BF16),
                        pltpu.VMEM((n, QW), BF16)],
    )
    return pl.pallas_call(
        functools.partial(_mix_sample_body, nb=nb, s=s, wc=wc),
        grid_spec=grid_spec,
        out_shape=jax.ShapeDtypeStruct((n, D_MODEL), F32),
        compiler_params=_params(1),
        name="mix_sample",
    )(sinks, *args)


def kernel(x_prompt, x_sample, cache_k, cache_v, norm_ffn1, ffn1_w1, ffn1_w3, ffn1_w2, norm_mix, w_in, gm_norm, gm_ws, gm_bs, sinks, w_pa, w_pb, w_out, norm_ffn2, ffn2_w1, ffn2_w3, ffn2_w2, norm_final):
    depth = norm_ffn1.shape[0]
    b, t, _ = x_prompt.shape
    nb, s, _ = x_sample.shape
    tm = 512
    keep = min(WINDOW, t)
    row = lambda a: a.reshape(1, -1).astype(F32)
    gf = row(norm_final)

    xp = x_prompt.reshape(b * t, D_MODEL)
    xs = x_sample.reshape(nb * s, D_MODEL)
    kp_l, vp_l, ks_l, vs_l, gs_l = [], [], [], [], []
    for l in range(depth):
        last = l == depth - 1
        f1 = (row(norm_ffn1[l]), ffn1_w1[l].astype(BF16), ffn1_w3[l].astype(BF16), ffn1_w2[l].astype(BF16))
        f2 = (row(norm_ffn2[l]), ffn2_w1[l].astype(BF16), ffn2_w3[l].astype(BF16), ffn2_w2[l].astype(BF16))
        win = w_in[l].astype(BF16)
        gmn = row(gm_norm[l])
        nmix = row(norm_mix[l])
        wpa, wpb, wout = w_pa[l].astype(BF16), w_pb[l].astype(BF16), w_out[l].astype(BF16)
        snk = sinks[l].astype(F32)
        ws = gm_ws[l].astype(F32)
        bs = gm_bs[l].astype(F32)

        xp = _ffn(xp, *f1, None, tm)
        u, v, q, kv, ga, gb = _inproj(xp, nmix, win, gmn, tm, BF16)
        r3 = lambda a: a.reshape(b, t, a.shape[-1])
        xp = _mix_prompt(snk, r3(xp), r3(u), r3(v), r3(q), r3(kv), r3(ga), r3(gb),
                         ws, bs.T, wpa, wpb, wout, tm).reshape(b * t, D_MODEL)
        kv4 = kv.reshape(b, t, 2, N_KV_HEADS, HEAD_DIM)
        kp_l.append(kv4[:, t - keep:, 0])
        vp_l.append(kv4[:, t - keep:, 1])
        xp = _ffn(xp, *f2, gf if last else None, tm)

        n = nb * s
        xs = _ffn(xs, *f1, None, n)
        u, v, q, kv, ga, gb = _inproj(xs, nmix, win, gmn, n, F32)
        wc = cache_k.shape[2]
        xs = _mix_sample(snk, xs, u, v, q, kv,
                         cache_k[l].reshape(nb, wc, KVW), cache_v[l].reshape(nb, wc, KVW), ga, gb,
                         jnp.tile(ws[:, :s, :s], (1, nb, nb)), jnp.tile(bs[:, :s].T, (nb, 1)),
                         wpa, wpb, wout, nb, s)
        kv4 = kv.reshape(nb, s, 2, N_KV_HEADS, HEAD_DIM)
        ks_l.append(kv4[:, :, 0])
        vs_l.append(kv4[:, :, 1])
        gs_l.append(v.reshape(nb, s, GM_WIDTH))
        xs = _ffn(xs, *f2, gf if last else None, n)

    return (xp.reshape(b, t, D_MODEL), xs.reshape(nb, s, D_MODEL), jnp.stack(kp_l), jnp.stack(vp_l),
            jnp.stack(ks_l), jnp.stack(vs_l), jnp.stack(gs_l))
```

```python
import functools

import jax
import jax.numpy as jnp
from jax import lax
from jax.experimental import pallas as pl
from jax.experimental.pallas import tpu as pltpu

D_MODEL = 1024
PAST_LEN = 1024
CHUNK = 64
GM_CHUNK = 128
GM_GROUPS = 8
GM_WIDTH = 1024
GM_GROUP_DIM = GM_WIDTH // GM_GROUPS
N_HEADS = 16
N_KV_HEADS = 2
HEAD_DIM = 64
Q_PER_KV = N_HEADS // N_KV_HEADS
WINDOW = 128
WINDOW_CHUNKS = WINDOW // CHUNK
BAND = WINDOW + CHUNK
QW = N_HEADS * HEAD_DIM
KVW = N_KV_HEADS * HEAD_DIM
D_FF = 2816
EPS = 1e-6
NEG = -1e30
SCALE = HEAD_DIM ** -0.5

C_U = 0
C_V = GM_WIDTH
C_Q = 2 * GM_WIDTH
C_KV = C_Q + QW
C_GA = C_KV + 2 * KVW
C_GB = C_GA + D_MODEL
IN_COLS = C_GB + D_MODEL

LANES = 128
SLAB_HEADS = LANES // HEAD_DIM
KV_SLABS = Q_PER_KV // SLAB_HEADS
N_GROUPS = N_KV_HEADS * SLAB_HEADS
KEY_TILE = 2 * LANES
FF_TILE = 256
VMEM_LIMIT = 56 * 1024 * 1024

F32 = jnp.float32
BF16 = jnp.bfloat16


def _rms(x, g):
    return x * lax.rsqrt(jnp.mean(x * x, axis=-1, keepdims=True) + EPS) * g


def _dot(a, b):
    return jnp.dot(a, b, preferred_element_type=F32)


def _dot_nt(a, b):
    return lax.dot_general(a, b, (((1,), (1,)), ((), ())), preferred_element_type=F32)


def _const_spec(shape):
    zeros = (0,) * len(shape)
    return pl.BlockSpec(shape, lambda *_: zeros, pipeline_mode=pl.Buffered(1))


def _params(n_axes):
    return pltpu.CompilerParams(dimension_semantics=("arbitrary",) * n_axes,
                                vmem_limit_bytes=VMEM_LIMIT)


def _ffn_body(x_ref, g_ref, w1_ref, w3_ref, w2_ref, *rest, final):
    if final:
        gf_ref, o_ref = rest
    else:
        (o_ref,) = rest
    x = x_ref[...]
    h = _rms(x, g_ref[...]).astype(BF16)
    acc = jnp.zeros(x.shape, F32)
    for f in range(D_FF // FF_TILE):
        sl = slice(f * FF_TILE, (f + 1) * FF_TILE)
        a = _dot(h, w1_ref[:, sl])
        b = _dot(h, w3_ref[:, sl])
        gated = (a * jax.nn.sigmoid(a) * b).astype(BF16)
        acc = acc + _dot(gated, w2_ref[sl, :])
    y = x + 0.5 * acc
    if final:
        y = _rms(y, gf_ref[...])
    o_ref[...] = y


def _ffn(x, g, w1, w3, w2, gf, tm):
    n = x.shape[0]
    final = gf is not None
    row = pl.BlockSpec((tm, D_MODEL), lambda i: (i, 0))
    in_specs = [row, _const_spec((1, D_MODEL)), _const_spec((D_MODEL, D_FF)),
                _const_spec((D_MODEL, D_FF)), _const_spec((D_FF, D_MODEL))]
    args = [x, g, w1, w3, w2]
    if final:
        in_specs.append(_const_spec((1, D_MODEL)))
        args.append(gf)
    return pl.pallas_call(
        functools.partial(_ffn_body, final=final),
        grid=(n // tm,),
        in_specs=in_specs,
        out_specs=row,
        out_shape=jax.ShapeDtypeStruct((n, D_MODEL), F32),
        compiler_params=_params(1),
        name="ffn_final" if final else "ffn",
    )(*args)


def _inproj_body(x_ref, g_ref, w_ref, gm_ref, u_ref, v_ref, q_ref, kv_ref, ga_ref, gb_ref):
    h = _rms(x_ref[...], g_ref[...]).astype(BF16)

    def proj(lo, hi):
        return _dot(h, w_ref[:, lo:hi])

    u_ref[...] = jax.nn.gelu(proj(C_U, C_V)).astype(u_ref.dtype)
    v_ref[...] = _rms(jax.nn.gelu(proj(C_V, C_Q)), gm_ref[...]).astype(v_ref.dtype)
    q_ref[...] = (proj(C_Q, C_KV) * SCALE).astype(q_ref.dtype)
    kv_ref[...] = proj(C_KV, C_GA)
    ga_ref[...] = jax.nn.sigmoid(proj(C_GA, C_GB)).astype(ga_ref.dtype)
    gb_ref[...] = jax.nn.sigmoid(proj(C_GB, IN_COLS)).astype(gb_ref.dtype)


def _inproj(x, g, w_in, gm, tm, v_dtype):
    n = x.shape[0]
    row = lambda w: pl.BlockSpec((tm, w), lambda i: (i, 0))
    wide = jax.ShapeDtypeStruct((n, D_MODEL), BF16)
    return pl.pallas_call(
        _inproj_body,
        grid=(n // tm,),
        in_specs=[row(D_MODEL), _const_spec((1, D_MODEL)), _const_spec((D_MODEL, IN_COLS)),
                  _const_spec((1, GM_WIDTH))],
        out_specs=[row(GM_WIDTH), row(GM_WIDTH), row(QW), row(2 * KVW), row(D_MODEL), row(D_MODEL)],
        out_shape=[wide, jax.ShapeDtypeStruct((n, GM_WIDTH), v_dtype), wide,
                   jax.ShapeDtypeStruct((n, 2 * KVW), F32), wide, wide],
        compiler_params=_params(1),
        name="inproj",
    )(x, g, w_in, gm)


def _stage_kv(kv, kst_ref, vst_ref):
    rows = kv.shape[0]
    k2 = kv[:, :KVW]
    v2 = kv[:, KVW:]
    low = lax.broadcasted_iota(jnp.int32, k2.shape, 1) < HEAD_DIM
    for src, dst in ((k2, kst_ref), (v2, vst_ref)):
        rolled = pltpu.roll(src, HEAD_DIM, 1)
        dst[0, :rows] = jnp.where(low, src, 0.0).astype(BF16)
        dst[1, :rows] = jnp.where(low, 0.0, rolled).astype(BF16)
        dst[2, :rows] = jnp.where(low, rolled, 0.0).astype(BF16)
        dst[3, :rows] = jnp.where(low, 0.0, src).astype(BF16)
    pad = kst_ref.shape[1] - rows
    if pad:
        kst_ref[:, rows:, :] = jnp.zeros((kst_ref.shape[0], pad, LANES), BF16)


def _stage_sink_fill(sinks_ref, fill_ref, rows_per_slab, n_keys):
    shape = fill_ref.shape[1:]
    row = lax.broadcasted_iota(jnp.int32, shape, 0)
    lane = lax.broadcasted_iota(jnp.int32, shape, 1)
    for kh in range(N_KV_HEADS):
        for j in range(SLAB_HEADS):
            head = lambda p: kh * Q_PER_KV + SLAB_HEADS * p + j
            snk = jnp.full(shape, sinks_ref[head(KV_SLABS - 1)], F32)
            for p in range(KV_SLABS - 2, -1, -1):
                snk = jnp.where(row < (p + 1) * rows_per_slab, sinks_ref[head(p)], snk)
            fill_ref[kh * SLAB_HEADS + j] = jnp.where(lane == n_keys - LANES, snk, NEG)


def _attend(groups, n_keys):
    scores = [_dot_nt(g[0], g[1]) for g in groups]
    probs, rdens = [], []
    for s, (_, _, _, fill, mask_lo, mask_hi) in zip(scores, groups):
        s_lo, s_hi = s[:, :LANES], s[:, LANES:]
        if mask_lo is not None:
            s_lo = jnp.where(mask_lo, s_lo, NEG)
        s_hi = jnp.where(mask_hi, s_hi, fill)
        m = jnp.max(jnp.maximum(s_lo, s_hi), axis=1, keepdims=True)
        e_lo = jnp.exp(s_lo - m)
        e_hi = jnp.exp(s_hi - m)
        rdens.append(1.0 / jnp.sum(e_lo + e_hi, axis=1, keepdims=True))
        probs.append(jnp.concatenate([e_lo, e_hi], axis=1).astype(BF16))
    return [_dot(p[:, :n_keys], g[2]) * r for p, r, g in zip(probs, rdens, groups)]


def _merge(x, oa, ob, ga, gb, wpa_ref, wpb_ref, wout_ref):
    ya = _dot(oa, wpa_ref[...])
    yb = _dot(ob, wpb_ref[...])
    m = (ga.astype(F32) * ya + gb.astype(F32) * yb).astype(BF16)
    return x + _dot(m, wout_ref[...])


def _div(x, d):
    assert d & (d - 1) == 0
    return lax.shift_right_logical(x, d.bit_length() - 1)


def _mod(x, d):
    assert d & (d - 1) == 0
    return x & (d - 1)


def _chunk_causal(i, j):
    return _div(i, CHUNK) >= _div(j, CHUNK)


def _mix_prompt_body(sinks_ref, x_ref, u_ref, v_ref, q_ref, kvp_ref, kvc_ref, ga_ref, gb_ref,
                     ws_ref, bst_ref, wpa_ref, wpb_ref, wout_ref, o_ref,
                     kst_ref, vst_ref, fill_ref, oa_ref, ob_ref, *, tm):
    i = pl.program_id(1)

    n_gm = tm // GM_CHUNK
    ri = lax.broadcasted_iota(jnp.int32, (GM_CHUNK, GM_CHUNK), 0)
    ci = lax.broadcasted_iota(jnp.int32, (GM_CHUNK, GM_CHUNK), 1)
    gm_mask = _chunk_causal(ri, ci).astype(F32)
    for g in range(GM_GROUPS):
        cols = slice(g * GM_GROUP_DIM, (g + 1) * GM_GROUP_DIM)
        w = (ws_ref[g] * gm_mask).astype(BF16)
        rhs = jnp.concatenate(
            [v_ref[0, c * GM_CHUNK:(c + 1) * GM_CHUNK, cols] for c in range(n_gm)], axis=1)
        sg = _dot(w, rhs) + bst_ref[:, g:g + 1]
        for c in range(n_gm):
            rows = slice(c * GM_CHUNK, (c + 1) * GM_CHUNK)
            oa_ref[rows, cols] = (u_ref[0, rows, cols].astype(F32)
                                  * sg[:, c * GM_CHUNK:(c + 1) * GM_CHUNK]).astype(BF16)

    _stage_kv(jnp.concatenate([kvp_ref[0], kvc_ref[0]], axis=0), kst_ref, vst_ref)
    _stage_sink_fill(sinks_ref, fill_ref, CHUNK, BAND)
    lane = lax.broadcasted_iota(jnp.int32, (1, LANES), 1)
    mask_hi = lane < BAND - LANES
    slab = lambda kh, p: slice((kh * KV_SLABS + p) * LANES, (kh * KV_SLABS + p + 1) * LANES)
    for c in range(tm // CHUNK):
        r0 = c * CHUNK
        mask_lo = (i * tm + r0 - WINDOW + lane >= 0) if r0 < WINDOW else None
        groups = []
        for kh in range(N_KV_HEADS):
            lhs = jnp.concatenate(
                [q_ref[0, r0:r0 + CHUNK, slab(kh, p)] for p in range(KV_SLABS)], axis=0)
            for j in range(SLAB_HEADS):
                g = kh * SLAB_HEADS + j
                groups.append((lhs, kst_ref[g, r0:r0 + KEY_TILE, :], vst_ref[g, r0:r0 + BAND, :],
                               fill_ref[g], mask_lo, mask_hi))
        outs = _attend(groups, BAND)
        for kh in range(N_KV_HEADS):
            o = outs[kh * SLAB_HEADS]
            for j in range(1, SLAB_HEADS):
                o = o + outs[kh * SLAB_HEADS + j]
            for p in range(KV_SLABS):
                ob_ref[r0:r0 + CHUNK, slab(kh, p)] = o[p * CHUNK:(p + 1) * CHUNK].astype(BF16)

    o_ref[0] = _merge(x_ref[0], oa_ref[...], ob_ref[...], ga_ref[0], gb_ref[0],
                      wpa_ref, wpb_ref, wout_ref)


def _mix_prompt(sinks, x, u, v, q, kv, ga, gb, ws, bst, wpa, wpb, wout, tm):
    b, t, _ = x.shape
    blk = lambda w: pl.BlockSpec((1, tm, w), lambda bi, i, s: (bi, i, 0))
    prev = pl.BlockSpec((1, WINDOW, 2 * KVW),
                        lambda bi, i, s: (bi, jnp.maximum(i * (tm // WINDOW) - 1, 0), 0))
    grid_spec = pltpu.PrefetchScalarGridSpec(
        num_scalar_prefetch=1,
        grid=(b, t // tm),
        in_specs=[blk(D_MODEL), blk(GM_WIDTH), blk(GM_WIDTH), blk(QW), prev, blk(2 * KVW),
                  blk(D_MODEL), blk(D_MODEL),
                  _const_spec((GM_GROUPS, GM_CHUNK, GM_CHUNK)), _const_spec((GM_CHUNK, GM_GROUPS)),
                  _const_spec((GM_WIDTH, D_MODEL)), _const_spec((QW, D_MODEL)),
                  _const_spec((D_MODEL, D_MODEL))],
        out_specs=blk(D_MODEL),
        scratch_shapes=[pltpu.VMEM((N_GROUPS, tm - CHUNK + KEY_TILE, LANES), BF16),
                        pltpu.VMEM((N_GROUPS, tm + WINDOW, LANES), BF16),
                        pltpu.VMEM((N_GROUPS, KV_SLABS * CHUNK, LANES), F32),
                        pltpu.VMEM((tm, GM_WIDTH), BF16),
                        pltpu.VMEM((tm, QW), BF16)],
    )
    return pl.pallas_call(
        functools.partial(_mix_prompt_body, tm=tm),
        grid_spec=grid_spec,
        out_shape=jax.ShapeDtypeStruct((b, t, D_MODEL), F32),
        compiler_params=_params(2),
        name="mix_prompt",
    )(sinks, x, u, v, q, kv, kv, ga, gb, ws, bst, wpa, wpb, wout)


def _mix_sample_body(sinks_ref, x_ref, u_ref, v_ref, q_ref, kv_ref, ck_ref, cv_ref, ga_ref, gb_ref,
                     wst_ref, bst_ref, wpa_ref, wpb_ref, wout_ref, o_ref,
                     kst_ref, vst_ref, fill_ref, oa_ref, ob_ref, *, nb, s, wc):
    n = nb * s
    ri = lax.broadcasted_iota(jnp.int32, (n, n), 0)
    ci = lax.broadcasted_iota(jnp.int32, (n, n), 1)
    gm_mask = ((_div(ri, s) == _div(ci, s)) & _chunk_causal(_mod(ri, s), _mod(ci, s))).astype(F32)
    for g in range(GM_GROUPS):
        cols = slice(g * GM_GROUP_DIM, (g + 1) * GM_GROUP_DIM)
        w = (wst_ref[g] * gm_mask).astype(BF16)
        sg = _dot(w, v_ref[:, cols].astype(BF16)) + bst_ref[:, g:g + 1]
        oa_ref[:, cols] = (u_ref[:, cols].astype(F32) * sg).astype(BF16)

    n_keys = wc + s
    _stage_sink_fill(sinks_ref, fill_ref, s, n_keys)
    rows = KV_SLABS * s
    q_pos = PAST_LEN + _mod(lax.broadcasted_iota(jnp.int32, (rows, KEY_TILE), 0), s)
    col = lax.broadcasted_iota(jnp.int32, (rows, KEY_TILE), 1)
    dist = _div(q_pos, CHUNK) - _div(PAST_LEN - wc + col, CHUNK)
    valid = (dist >= 0) & (dist <= WINDOW_CHUNKS) & (col < n_keys)
    slab = lambda kh, p: slice((kh * KV_SLABS + p) * LANES, (kh * KV_SLABS + p + 1) * LANES)
    for b in range(nb):
        r = slice(b * s, (b + 1) * s)
        new = kv_ref[r, :]
        kv = jnp.concatenate(
            [jnp.concatenate([ck_ref[b], new[:, :KVW]], axis=0),
             jnp.concatenate([cv_ref[b], new[:, KVW:]], axis=0)], axis=1)
        _stage_kv(kv, kst_ref, vst_ref)
        groups = []
        for kh in range(N_KV_HEADS):
            lhs = jnp.concatenate([q_ref[r, slab(kh, p)] for p in range(KV_SLABS)], axis=0)
            for j in range(SLAB_HEADS):
                g = kh * SLAB_HEADS + j
                groups.append((lhs, kst_ref[g], vst_ref[g], fill_ref[g],
                               valid[:, :LANES], valid[:, LANES:]))
        outs = _attend(groups, n_keys)
        for kh in range(N_KV_HEADS):
            o = outs[kh * SLAB_HEADS]
            for j in range(1, SLAB_HEADS):
                o = o + outs[kh * SLAB_HEADS + j]
            for p in range(KV_SLABS):
                ob_ref[r, slab(kh, p)] = o[p * s:(p + 1) * s].astype(BF16)

    o_ref[...] = _merge(x_ref[...], oa_ref[...], ob_ref[...], ga_ref[...], gb_ref[...],
                        wpa_ref, wpb_ref, wout_ref)


def _mix_sample(sinks, x, u, v, q, kv, ck, cv, ga, gb, wst, bst, wpa, wpb, wout, nb, s):
    n = nb * s
    wc = ck.shape[1]
    full = lambda a: pl.BlockSpec(a.shape, lambda i, sk, nd=a.ndim: (0,) * nd)
    args = (x, u, v, q, kv, ck, cv, ga, gb, wst, bst, wpa, wpb, wout)
    grid_spec = pltpu.PrefetchScalarGridSpec(
        num_scalar_prefetch=1,
        grid=(1,),
        in_specs=[full(a) for a in args],
        out_specs=pl.BlockSpec((n, D_MODEL), lambda i, sk: (0, 0)),
        scratch_shapes=[pltpu.VMEM((N_GROUPS, KEY_TILE, LANES), BF16),
                        pltpu.VMEM((N_GROUPS, wc + s, LANES), BF16),
                        pltpu.VMEM((N_GROUPS, KV_SLABS * s, LANES), F32),
                        pltpu.VMEM((n, GM_WIDTH), BF16),
                        pltpu.VMEM((n, QW), BF16)],
    )
    return pl.pallas_call(
        functools.partial(_mix_sample_body, nb=nb, s=s, wc=wc),
        grid_spec=grid_spec,
        out_shape=jax.ShapeDtypeStruct((n, D_MODEL), F32),
        compiler_params=_params(1),
        name="mix_sample",
    )(sinks, *args)


def kernel(x_prompt, x_sample, cache_k, cache_v, norm_ffn1, ffn1_w1, ffn1_w3, ffn1_w2, norm_mix, w_in, gm_norm, gm_ws, gm_bs, sinks, w_pa, w_pb, w_out, norm_ffn2, ffn2_w1, ffn2_w3, ffn2_w2, norm_final):
    depth = norm_ffn1.shape[0]
    b, t, _ = x_prompt.shape
    nb, s, _ = x_sample.shape
    tm = 512
    keep = min(WINDOW, t)
    row = lambda a: a.reshape(1, -1).astype(F32)
    gf = row(norm_final)

    xp = x_prompt.reshape(b * t, D_MODEL)
    xs = x_sample.reshape(nb * s, D_MODEL)
    kp_l, vp_l, ks_l, vs_l, gs_l = [], [], [], [], []
    for l in range(depth):
        last = l == depth - 1
        f1 = (row(norm_ffn1[l]), ffn1_w1[l].astype(BF16), ffn1_w3[l].astype(BF16), ffn1_w2[l].astype(BF16))
        f2 = (row(norm_ffn2[l]), ffn2_w1[l].astype(BF16), ffn2_w3[l].astype(BF16), ffn2_w2[l].astype(BF16))
        win = w_in[l].astype(BF16)
        gmn = row(gm_norm[l])
        nmix = row(norm_mix[l])
        wpa, wpb, wout = w_pa[l].astype(BF16), w_pb[l].astype(BF16), w_out[l].astype(BF16)
        snk = sinks[l].astype(F32)
        ws = gm_ws[l].astype(F32)
        bs = gm_bs[l].astype(F32)

        xp = _ffn(xp, *f1, None, tm)
        u, v, q, kv, ga, gb = _inproj(xp, nmix, win, gmn, tm, BF16)
        r3 = lambda a: a.reshape(b, t, a.shape[-1])
        xp = _mix_prompt(snk, r3(xp), r3(u), r3(v), r3(q), r3(kv), r3(ga), r3(gb),
                         ws, bs.T, wpa, wpb, wout, tm).reshape(b * t, D_MODEL)
        kv4 = kv.reshape(b, t, 2, N_KV_HEADS, HEAD_DIM)
        kp_l.append(kv4[:, t - keep:, 0])
        vp_l.append(kv4[:, t - keep:, 1])
        xp = _ffn(xp, *f2, gf if last else None, tm)

        n = nb * s
        xs = _ffn(xs, *f1, None, n)
        u, v, q, kv, ga, gb = _inproj(xs, nmix, win, gmn, n, F32)
        wc = cache_k.shape[2]
        xs = _mix_sample(snk, xs, u, v, q, kv,
                         cache_k[l].reshape(nb, wc, KVW), cache_v[l].reshape(nb, wc, KVW), ga, gb,
                         jnp.tile(ws[:, :s, :s], (1, nb, nb)), jnp.tile(bs[:, :s].T, (nb, 1)),
                         wpa, wpb, wout, nb, s)
        kv4 = kv.reshape(nb, s, 2, N_KV_HEADS, HEAD_DIM)
        ks_l.append(kv4[:, :, 0])
        vs_l.append(kv4[:, :, 1])
        gs_l.append(v.reshape(nb, s, GM_WIDTH))
        xs = _ffn(xs, *f2, gf if last else None, n)

    return (xp.reshape(b, t, D_MODEL), xs.reshape(nb, s, D_MODEL), jnp.stack(kp_l), jnp.stack(vp_l),
            jnp.stack(ks_l), jnp.stack(vs_l), jnp.stack(gs_l))
```

```python
import functools

import jax
import jax.numpy as jnp
from jax import lax
from jax.experimental import pallas as pl
from jax.experimental.pallas import tpu as pltpu

D_MODEL = 1024
PAST_LEN = 1024
CHUNK = 64
GM_CHUNK = 128
GM_GROUPS = 8
GM_WIDTH = 1024
GM_GROUP_DIM = GM_WIDTH // GM_GROUPS
N_HEADS = 16
N_KV_HEADS = 2
HEAD_DIM = 64
Q_PER_KV = N_HEADS // N_KV_HEADS
WINDOW = 128
WINDOW_CHUNKS = WINDOW // CHUNK
BAND = WINDOW + CHUNK
QW = N_HEADS * HEAD_DIM
KVW = N_KV_HEADS * HEAD_DIM
D_FF = 2816
EPS = 1e-6
NEG = -1e30
SCALE = HEAD_DIM ** -0.5

C_U = 0
C_V = GM_WIDTH
C_Q = 2 * GM_WIDTH
C_KV = C_Q + QW
C_GA = C_KV + 2 * KVW
C_GB = C_GA + D_MODEL
IN_COLS = C_GB + D_MODEL

LANES = 128
SLAB_HEADS = LANES // HEAD_DIM
KV_SLABS = Q_PER_KV // SLAB_HEADS
N_GROUPS = N_KV_HEADS * SLAB_HEADS
KEY_TILE = 2 * LANES
FF_TILE = 256
VMEM_LIMIT = 56 * 1024 * 1024

F32 = jnp.float32
BF16 = jnp.bfloat16


def _rms(x, g):
    return x * lax.rsqrt(jnp.mean(x * x, axis=-1, keepdims=True) + EPS) * g


def _dot(a, b):
    return jnp.dot(a, b, preferred_element_type=F32)


def _dot_nt(a, b):
    return lax.dot_general(a, b, (((1,), (1,)), ((), ())), preferred_element_type=F32)


def _const_spec(shape):
    zeros = (0,) * len(shape)
    return pl.BlockSpec(shape, lambda *_: zeros, pipeline_mode=pl.Buffered(1))


def _params(n_axes):
    return pltpu.CompilerParams(dimension_semantics=("arbitrary",) * n_axes,
                                vmem_limit_bytes=VMEM_LIMIT)


def _ffn_body(x_ref, g_ref, w1_ref, w3_ref, w2_ref, *rest, final):
    if final:
        gf_ref, o_ref = rest
    else:
        (o_ref,) = rest
    x = x_ref[...]
    h = _rms(x, g_ref[...]).astype(BF16)
    acc = jnp.zeros(x.shape, F32)
    for f in range(D_FF // FF_TILE):
        sl = slice(f * FF_TILE, (f + 1) * FF_TILE)
        a = _dot(h, w1_ref[:, sl])
        b = _dot(h, w3_ref[:, sl])
        gated = (a * jax.nn.sigmoid(a) * b).astype(BF16)
        acc = acc + _dot(gated, w2_ref[sl, :])
    y = x + 0.5 * acc
    if final:
        y = _rms(y, gf_ref[...])
    o_ref[...] = y


def _ffn(x, g, w1, w3, w2, gf, tm):
    n = x.shape[0]
    final = gf is not None
    row = pl.BlockSpec((tm, D_MODEL), lambda i: (i, 0))
    in_specs = [row, _const_spec((1, D_MODEL)), _const_spec((D_MODEL, D_FF)),
                _const_spec((D_MODEL, D_FF)), _const_spec((D_FF, D_MODEL))]
    args = [x, g, w1, w3, w2]
    if final:
        in_specs.append(_const_spec((1, D_MODEL)))
        args.append(gf)
    return pl.pallas_call(
        functools.partial(_ffn_body, final=final),
        grid=(n // tm,),
        in_specs=in_specs,
        out_specs=row,
        out_shape=jax.ShapeDtypeStruct((n, D_MODEL), F32),
        compiler_params=_params(1),
        name="ffn_final" if final else "ffn",
    )(*args)


def _inproj_body(x_ref, g_ref, w_ref, gm_ref, u_ref, v_ref, q_ref, kv_ref, ga_ref, gb_ref):
    h = _rms(x_ref[...], g_ref[...]).astype(BF16)

    def proj(lo, hi):
        return _dot(h, w_ref[:, lo:hi])

    u_ref[...] = jax.nn.gelu(proj(C_U, C_V)).astype(u_ref.dtype)
    v_ref[...] = _rms(jax.nn.gelu(proj(C_V, C_Q)), gm_ref[...]).astype(v_ref.dtype)
    q_ref[...] = (proj(C_Q, C_KV) * SCALE).astype(q_ref.dtype)
    kv_ref[...] = proj(C_KV, C_GA)
    ga_ref[...] = jax.nn.sigmoid(proj(C_GA, C_GB)).astype(ga_ref.dtype)
    gb_ref[...] = jax.nn.sigmoid(proj(C_GB, IN_COLS)).astype(gb_ref.dtype)


def _inproj(x, g, w_in, gm, tm, v_dtype):
    n = x.shape[0]
    row = lambda w: pl.BlockSpec((tm, w), lambda i: (i, 0))
    wide = jax.ShapeDtypeStruct((n, D_MODEL), BF16)
    return pl.pallas_call(
        _inproj_body,
        grid=(n // tm,),
        in_specs=[row(D_MODEL), _const_spec((1, D_MODEL)), _const_spec((D_MODEL, IN_COLS)),
                  _const_spec((1, GM_WIDTH))],
        out_specs=[row(GM_WIDTH), row(GM_WIDTH), row(QW), row(2 * KVW), row(D_MODEL), row(D_MODEL)],
        out_shape=[wide, jax.ShapeDtypeStruct((n, GM_WIDTH), v_dtype), wide,
                   jax.ShapeDtypeStruct((n, 2 * KVW), F32), wide, wide],
        compiler_params=_params(1),
        name="inproj",
    )(x, g, w_in, gm)


def _stage_kv(kv, kst_ref, vst_ref):
    rows = kv.shape[0]
    k2 = kv[:, :KVW]
    v2 = kv[:, KVW:]
    low = lax.broadcasted_iota(jnp.int32, k2.shape, 1) < HEAD_DIM
    for src, dst in ((k2, kst_ref), (v2, vst_ref)):
        rolled = pltpu.roll(src, HEAD_DIM, 1)
        dst[0, :rows] = jnp.where(low, src, 0.0).astype(BF16)
        dst[1, :rows] = jnp.where(low, 0.0, rolled).astype(BF16)
        dst[2, :rows] = jnp.where(low, rolled, 0.0).astype(BF16)
        dst[3, :rows] = jnp.where(low, 0.0, src).astype(BF16)
    pad = kst_ref.shape[1] - rows
    if pad:
        kst_ref[:, rows:, :] = jnp.zeros((kst_ref.shape[0], pad, LANES), BF16)


def _stage_sink_fill(sinks_ref, fill_ref, rows_per_slab, n_keys):
    shape = fill_ref.shape[1:]
    row = lax.broadcasted_iota(jnp.int32, shape, 0)
    lane = lax.broadcasted_iota(jnp.int32, shape, 1)
    for kh in range(N_KV_HEADS):
        for j in range(SLAB_HEADS):
            head = lambda p: kh * Q_PER_KV + SLAB_HEADS * p + j
            snk = jnp.full(shape, sinks_ref[head(KV_SLABS - 1)], F32)
            for p in range(KV_SLABS - 2, -1, -1):
                snk = jnp.where(row < (p + 1) * rows_per_slab, sinks_ref[head(p)], snk)
            fill_ref[kh * SLAB_HEADS + j] = jnp.where(lane == n_keys - LANES, snk, NEG)


def _attend(groups, n_keys):
    scores = [_dot_nt(g[0], g[1]) for g in groups]
    probs, rdens = [], []
    for s, (_, _, _, fill, mask_lo, mask_hi) in zip(scores, groups):
        s_lo, s_hi = s[:, :LANES], s[:, LANES:]
        if mask_lo is not None:
            s_lo = jnp.where(mask_lo, s_lo, NEG)
        s_hi = jnp.where(mask_hi, s_hi, fill)
        m = jnp.max(jnp.maximum(s_lo, s_hi), axis=1, keepdims=True)
        e_lo = jnp.exp(s_lo - m)
        e_hi = jnp.exp(s_hi - m)
        rdens.append(1.0 / jnp.sum(e_lo + e_hi, axis=1, keepdims=True))
        probs.append(jnp.concatenate([e_lo, e_hi], axis=1).astype(BF16))
    return [_dot(p[:, :n_keys], g[2]) * r for p, r, g in zip(probs, rdens, groups)]


def _merge(x, oa, ob, ga, gb, wpa_ref, wpb_ref, wout_ref):
    ya = _dot(oa, wpa_ref[...])
    yb = _dot(ob, wpb_ref[...])
    m = (ga.astype(F32) * ya + gb.astype(F32) * yb).astype(BF16)
    return x + _dot(m, wout_ref[...])


def _div(x, d):
    assert d & (d - 1) == 0
    return lax.shift_right_logical(x, d.bit_length() - 1)


def _mod(x, d):
    assert d & (d - 1) == 0
    return x & (d - 1)


def _chunk_causal(i, j):
    return _div(i, CHUNK) >= _div(j, CHUNK)


def _mix_prompt_body(sinks_ref, x_ref, u_ref, v_ref, q_ref, kvp_ref, kvc_ref, ga_ref, gb_ref,
                     ws_ref, bst_ref, wpa_ref, wpb_ref, wout_ref, o_ref,
                     kst_ref, vst_ref, fill_ref, oa_ref, ob_ref, *, tm):
    i = pl.program_id(1)

    n_gm = tm // GM_CHUNK
    ri = lax.broadcasted_iota(jnp.int32, (GM_CHUNK, GM_CHUNK), 0)
    ci = lax.broadcasted_iota(jnp.int32, (GM_CHUNK, GM_CHUNK), 1)
    gm_mask = _chunk_causal(ri, ci).astype(F32)
    for g in range(GM_GROUPS):
        cols = slice(g * GM_GROUP_DIM, (g + 1) * GM_GROUP_DIM)
        w = (ws_ref[g] * gm_mask).astype(BF16)
        rhs = jnp.concatenate(
            [v_ref[0, c * GM_CHUNK:(c + 1) * GM_CHUNK, cols] for c in range(n_gm)], axis=1)
        sg = _dot(w, rhs) + bst_ref[:, g:g + 1]
        for c in range(n_gm):
            rows = slice(c * GM_CHUNK, (c + 1) * GM_CHUNK)
            oa_ref[rows, cols] = (u_ref[0, rows, cols].astype(F32)
                                  * sg[:, c * GM_CHUNK:(c + 1) * GM_CHUNK]).astype(BF16)

    _stage_kv(jnp.concatenate([kvp_ref[0], kvc_ref[0]], axis=0), kst_ref, vst_ref)
    _stage_sink_fill(sinks_ref, fill_ref, CHUNK, BAND)
    lane = lax.broadcasted_iota(jnp.int32, (1, LANES), 1)
    mask_hi = lane < BAND - LANES
    slab = lambda kh, p: slice((kh * KV_SLABS + p) * LANES, (kh * KV_SLABS + p + 1) * LANES)
    for c in range(tm // CHUNK):
        r0 = c * CHUNK
        mask_lo = (i * tm + r0 - WINDOW + lane >= 0) if r0 < WINDOW else None
        groups = []
        for kh in range(N_KV_HEADS):
            lhs = jnp.concatenate(
                [q_ref[0, r0:r0 + CHUNK, slab(kh, p)] for p in range(KV_SLABS)], axis=0)
            for j in range(SLAB_HEADS):
                g = kh * SLAB_HEADS + j
                groups.append((lhs, kst_ref[g, r0:r0 + KEY_TILE, :], vst_ref[g, r0:r0 + BAND, :],
                               fill_ref[g], mask_lo, mask_hi))
        outs = _attend(groups, BAND)
        for kh in range(N_KV_HEADS):
            o = outs[kh * SLAB_HEADS]
            for j in range(1, SLAB_HEADS):
                o = o + outs[kh * SLAB_HEADS + j]
            for p in range(KV_SLABS):
                ob_ref[r0:r0 + CHUNK, slab(kh, p)] = o[p * CHUNK:(p + 1) * CHUNK].astype(BF16)

    o_ref[0] = _merge(x_ref[0], oa_ref[...], ob_ref[...], ga_ref[0], gb_ref[0],
                      wpa_ref, wpb_ref, wout_ref)


def _mix_prompt(sinks, x, u, v, q, kv, ga, gb, ws, bst, wpa, wpb, wout, tm):
    b, t, _ = x.shape
    blk = lambda w: pl.BlockSpec((1, tm, w), lambda bi, i, s: (bi, i, 0))
    prev = pl.BlockSpec((1, WINDOW, 2 * KVW),
                        lambda bi, i, s: (bi, jnp.maximum(i * (tm // WINDOW) - 1, 0), 0))
    grid_spec = pltpu.PrefetchScalarGridSpec(
        num_scalar_prefetch=1,
        grid=(b, t // tm),
        in_specs=[blk(D_MODEL), blk(GM_WIDTH), blk(GM_WIDTH), blk(QW), prev, blk(2 * KVW),
                  blk(D_MODEL), blk(D_MODEL),
                  _const_spec((GM_GROUPS, GM_CHUNK, GM_CHUNK)), _const_spec((GM_CHUNK, GM_GROUPS)),
                  _const_spec((GM_WIDTH, D_MODEL)), _const_spec((QW, D_MODEL)),
                  _const_spec((D_MODEL, D_MODEL))],
        out_specs=blk(D_MODEL),
        scratch_shapes=[pltpu.VMEM((N_GROUPS, tm - CHUNK + KEY_TILE, LANES), BF16),
                        pltpu.VMEM((N_GROUPS, tm + WINDOW, LANES), BF16),
                        pltpu.VMEM((N_GROUPS, KV_SLABS * CHUNK, LANES), F32),
                        pltpu.VMEM((tm, GM_WIDTH), BF16),
                        pltpu.VMEM((tm, QW), BF16)],
    )
    return pl.pallas_call(
        functools.partial(_mix_prompt_body, tm=tm),
        grid_spec=grid_spec,
        out_shape=jax.ShapeDtypeStruct((b, t, D_MODEL), F32),
        compiler_params=_params(2),
        name="mix_prompt",
    )(sinks, x, u, v, q, kv, kv, ga, gb, ws, bst, wpa, wpb, wout)


def _mix_sample_body(sinks_ref, x_ref, u_ref, v_ref, q_ref, kv_ref, ck_ref, cv_ref, ga_ref, gb_ref,
                     wst_ref, bst_ref, wpa_ref, wpb_ref, wout_ref, o_ref,
                     kst_ref, vst_ref, fill_ref, oa_ref, ob_ref, *, nb, s, wc):
    n = nb * s
    ri = lax.broadcasted_iota(jnp.int32, (n, n), 0)
    ci = lax.broadcasted_iota(jnp.int32, (n, n), 1)
    gm_mask = ((_div(ri, s) == _div(ci, s)) & _chunk_causal(_mod(ri, s), _mod(ci, s))).astype(F32)
    for g in range(GM_GROUPS):
        cols = slice(g * GM_GROUP_DIM, (g + 1) * GM_GROUP_DIM)
        w = (wst_ref[g] * gm_mask).astype(BF16)
        sg = _dot(w, v_ref[:, cols].astype(BF16)) + bst_ref[:, g:g + 1]
        oa_ref[:, cols] = (u_ref[:, cols].astype(F32) * sg).astype(BF16)

    n_keys = wc + s
    _stage_sink_fill(sinks_ref, fill_ref, s, n_keys)
    rows = KV_SLABS * s
    q_pos = PAST_LEN + _mod(lax.broadcasted_iota(jnp.int32, (rows, KEY_TILE), 0), s)
    col = lax.broadcasted_iota(jnp.int32, (rows, KEY_TILE), 1)
    dist = _div(q_pos, CHUNK) - _div(PAST_LEN - wc + col, CHUNK)
    valid = (dist >= 0) & (dist <= WINDOW_CHUNKS) & (col < n_keys)
    slab = lambda kh, p: slice((kh * KV_SLABS + p) * LANES, (kh * KV_SLABS + p + 1) * LANES)
    for b in range(nb):
        r = slice(b * s, (b + 1) * s)
        new = kv_ref[r, :]
        kv = jnp.concatenate(
            [jnp.concatenate([ck_ref[b], new[:, :KVW]], axis=0),
             jnp.concatenate([cv_ref[b], new[:, KVW:]], axis=0)], axis=1)
        _stage_kv(kv, kst_ref, vst_ref)
        groups = []
        for kh in range(N_KV_HEADS):
            lhs = jnp.concatenate([q_ref[r, slab(kh, p)] for p in range(KV_SLABS)], axis=0)
            for j in range(SLAB_HEADS):
                g = kh * SLAB_HEADS + j
                groups.append((lhs, kst_ref[g], vst_ref[g], fill_ref[g],
                               valid[:, :LANES], valid[:, LANES:]))
        outs = _attend(groups, n_keys)
        for kh in range(N_KV_HEADS):
            o = outs[kh * SLAB_HEADS]
            for j in range(1, SLAB_HEADS):
                o = o + outs[kh * SLAB_HEADS + j]
            for p in range(KV_SLABS):
                ob_ref[r, slab(kh, p)] = o[p * s:(p + 1) * s].astype(BF16)

    o_ref[...] = _merge(x_ref[...], oa_ref[...], ob_ref[...], ga_ref[...], gb_ref[...],
                        wpa_ref, wpb_ref, wout_ref)


def _mix_sample(sinks, x, u, v, q, kv, ck, cv, ga, gb, wst, bst, wpa, wpb, wout, nb, s):
    n = nb * s
    wc = ck.shape[1]
    full = lambda a: pl.BlockSpec(a.shape, lambda i, sk, nd=a.ndim: (0,) * nd)
    args = (x, u, v, q, kv, ck, cv, ga, gb, wst, bst, wpa, wpb, wout)
    grid_spec = pltpu.PrefetchScalarGridSpec(
        num_scalar_prefetch=1,
        grid=(1,),
        in_specs=[full(a) for a in args],
        out_specs=pl.BlockSpec((n, D_MODEL), lambda i, sk: (0, 0)),
        scratch_shapes=[pltpu.VMEM((N_GROUPS, KEY_TILE, LANES), BF16),
                        pltpu.VMEM((N_GROUPS, wc + s, LANES), BF16),
                        pltpu.VMEM((N_GROUPS, KV_SLABS * s, LANES), F32),
                        pltpu.VMEM((n, GM_WIDTH), BF16),
                        pltpu.VMEM((n, QW), BF16)],
    )
    return pl.pallas_call(
        functools.partial(_mix_sample_body, nb=nb, s=s, wc=wc),
        grid_spec=grid_spec,
        out_shape=jax.ShapeDtypeStruct((n, D_MODEL), F32),
        compiler_params=_params(1),
        name="mix_sample",
    )(sinks, *args)


def kernel(x_prompt, x_sample, cache_k, cache_v, norm_ffn1, ffn1_w1, ffn1_w3, ffn1_w2, norm_mix, w_in, gm_norm, gm_ws, gm_bs, sinks, w_pa, w_pb, w_out, norm_ffn2, ffn2_w1, ffn2_w3, ffn2_w2, norm_final):
    depth = norm_ffn1.shape[0]
    b, t, _ = x_prompt.shape
    nb, s, _ = x_sample.shape
    tm = 512
    tm_ffn = 1024
    keep = min(WINDOW, t)
    row = lambda a: a.reshape(1, -1).astype(F32)
    gf = row(norm_final)

    xp = x_prompt.reshape(b * t, D_MODEL)
    xs = x_sample.reshape(nb * s, D_MODEL)
    kp_l, vp_l, ks_l, vs_l, gs_l = [], [], [], [], []
    for l in range(depth):
        last = l == depth - 1
        f1 = (row(norm_ffn1[l]), ffn1_w1[l].astype(BF16), ffn1_w3[l].astype(BF16), ffn1_w2[l].astype(BF16))
        f2 = (row(norm_ffn2[l]), ffn2_w1[l].astype(BF16), ffn2_w3[l].astype(BF16), ffn2_w2[l].astype(BF16))
        win = w_in[l].astype(BF16)
        gmn = row(gm_norm[l])
        nmix = row(norm_mix[l])
        wpa, wpb, wout = w_pa[l].astype(BF16), w_pb[l].astype(BF16), w_out[l].astype(BF16)
        snk = sinks[l].astype(F32)
        ws = gm_ws[l].astype(F32)
        bs = gm_bs[l].astype(F32)

        xp = _ffn(xp, *f1, None, tm_ffn)
        u, v, q, kv, ga, gb = _inproj(xp, nmix, win, gmn, tm, BF16)
        r3 = lambda a: a.reshape(b, t, a.shape[-1])
        xp = _mix_prompt(snk, r3(xp), r3(u), r3(v), r3(q), r3(kv), r3(ga), r3(gb),
                         ws, bs.T, wpa, wpb, wout, tm).reshape(b * t, D_MODEL)
        kv_tail = kv.reshape(b, t, 2 * KVW)[:, t - keep:]
        kp_l.append(kv_tail[:, :, :KVW].reshape(b, keep, N_KV_HEADS, HEAD_DIM))
        vp_l.append(kv_tail[:, :, KVW:].reshape(b, keep, N_KV_HEADS, HEAD_DIM))
        xp = _ffn(xp, *f2, gf if last else None, tm_ffn)

        n = nb * s
        xs = _ffn(xs, *f1, None, n)
        u, v, q, kv, ga, gb = _inproj(xs, nmix, win, gmn, n, F32)
        wc = cache_k.shape[2]
        xs = _mix_sample(snk, xs, u, v, q, kv,
                         cache_k[l].reshape(nb, wc, KVW), cache_v[l].reshape(nb, wc, KVW), ga, gb,
                         jnp.tile(ws[:, :s, :s], (1, nb, nb)), jnp.tile(bs[:, :s].T, (nb, 1)),
                         wpa, wpb, wout, nb, s)
        kv4 = kv.reshape(nb, s, 2, N_KV_HEADS, HEAD_DIM)
        ks_l.append(kv4[:, :, 0])
        vs_l.append(kv4[:, :, 1])
        gs_l.append(v.reshape(nb, s, GM_WIDTH))
        xs = _ffn(xs, *f2, gf if last else None, n)

    return (xp.reshape(b, t, D_MODEL), xs.reshape(nb, s, D_MODEL), jnp.stack(kp_l), jnp.stack(vp_l),
            jnp.stack(ks_l), jnp.stack(vs_l), jnp.stack(gs_l))
```

```python
import functools

import jax
import jax.numpy as jnp
from jax import lax
from jax.experimental import pallas as pl
from jax.experimental.pallas import tpu as pltpu

D_MODEL = 1024
PAST_LEN = 1024
CHUNK = 64
GM_CHUNK = 128
GM_GROUPS = 8
GM_WIDTH = 1024
GM_GROUP_DIM = GM_WIDTH // GM_GROUPS
N_HEADS = 16
N_KV_HEADS = 2
HEAD_DIM = 64
Q_PER_KV = N_HEADS // N_KV_HEADS
WINDOW = 128
WINDOW_CHUNKS = WINDOW // CHUNK
BAND = WINDOW + CHUNK
QW = N_HEADS * HEAD_DIM
KVW = N_KV_HEADS * HEAD_DIM
D_FF = 2816
EPS = 1e-6
NEG = -1e30
SCALE = HEAD_DIM ** -0.5

C_U = 0
C_V = GM_WIDTH
C_Q = 2 * GM_WIDTH
C_KV = C_Q + QW
C_GA = C_KV + 2 * KVW
C_GB = C_GA + D_MODEL
IN_COLS = C_GB + D_MODEL

LANES = 128
SLAB_HEADS = LANES // HEAD_DIM
KV_SLABS = Q_PER_KV // SLAB_HEADS
N_GROUPS = N_KV_HEADS * SLAB_HEADS
KEY_TILE = 2 * LANES
FF_TILE = 256
PROJ_TILE = 512
VMEM_LIMIT = 56 * 1024 * 1024
TM_FFN = 1024
TM_MIX = 512

F32 = jnp.float32
BF16 = jnp.bfloat16


def _rms(x, g):
    return x * lax.rsqrt(jnp.mean(x * x, axis=-1, keepdims=True) + EPS) * g


def _dot(a, b):
    return jnp.dot(a, b, preferred_element_type=F32)


def _dot_nt(a, b):
    return lax.dot_general(a, b, (((1,), (1,)), ((), ())), preferred_element_type=F32)


def _const_spec(shape):
    zeros = (0,) * len(shape)
    return pl.BlockSpec(shape, lambda *_: zeros, pipeline_mode=pl.Buffered(1))


def _params():
    return pltpu.CompilerParams(dimension_semantics=("arbitrary",), vmem_limit_bytes=VMEM_LIMIT)


def _div(x, d):
    assert d & (d - 1) == 0
    return lax.shift_right_logical(x, d.bit_length() - 1)


def _mod(x, d):
    assert d & (d - 1) == 0
    return x & (d - 1)


def _chunk_causal(i, j):
    return _div(i, CHUNK) >= _div(j, CHUNK)


def _ffn_rows(x, g_ref, w1_ref, w3_ref, w2_ref, gf_ref):
    h = _rms(x, g_ref[...]).astype(BF16)
    acc = jnp.zeros(x.shape, F32)
    for f in range(D_FF // FF_TILE):
        sl = slice(f * FF_TILE, (f + 1) * FF_TILE)
        a = _dot(h, w1_ref[:, sl])
        b = _dot(h, w3_ref[:, sl])
        gated = (a * jax.nn.sigmoid(a) * b).astype(BF16)
        acc = acc + _dot(gated, w2_ref[sl, :])
    y = x + 0.5 * acc
    return y if gf_ref is None else _rms(y, gf_ref[...])


def _ffn_body(x_ref, xs_ref, g_ref, w1_ref, w3_ref, w2_ref, *rest, final, n_blocks):
    gf_ref = rest[0] if final else None
    o_ref, os_ref = rest[-2:]
    i = pl.program_id(0)

    @pl.when(i < n_blocks)
    def _():
        o_ref[...] = _ffn_rows(x_ref[...], g_ref, w1_ref, w3_ref, w2_ref, gf_ref)

    @pl.when(i == n_blocks)
    def _():
        os_ref[...] = _ffn_rows(xs_ref[...], g_ref, w1_ref, w3_ref, w2_ref, gf_ref)


def _ffn(x, xs, g, w1, w3, w2, gf):
    n, ns = x.shape[0], xs.shape[0]
    n_blocks = n // TM_FFN
    final = gf is not None
    row = pl.BlockSpec((TM_FFN, D_MODEL), lambda i: (jnp.minimum(i, n_blocks - 1), 0))
    in_specs = [row, _const_spec((ns, D_MODEL)), _const_spec((1, D_MODEL)),
                _const_spec((D_MODEL, D_FF)), _const_spec((D_MODEL, D_FF)),
                _const_spec((D_FF, D_MODEL))]
    args = [x, xs, g, w1, w3, w2]
    if final:
        in_specs.append(_const_spec((1, D_MODEL)))
        args.append(gf)
    return pl.pallas_call(
        functools.partial(_ffn_body, final=final, n_blocks=n_blocks),
        grid=(n_blocks + 1,),
        in_specs=in_specs,
        out_specs=[row, pl.BlockSpec((ns, D_MODEL), lambda i: (0, 0))],
        out_shape=[jax.ShapeDtypeStruct((n, D_MODEL), F32), jax.ShapeDtypeStruct((ns, D_MODEL), F32)],
        compiler_params=_params(),
        name="ffn_final" if final else "ffn",
    )(*args)


def _stage_kv(kv, kst_ref, vst_ref):
    rows = kv.shape[0]
    k2 = kv[:, :KVW]
    v2 = kv[:, KVW:]
    low = lax.broadcasted_iota(jnp.int32, k2.shape, 1) < HEAD_DIM
    for src, dst in ((k2, kst_ref), (v2, vst_ref)):
        rolled = pltpu.roll(src, HEAD_DIM, 1)
        dst[0, :rows] = jnp.where(low, src, 0.0).astype(BF16)
        dst[1, :rows] = jnp.where(low, 0.0, rolled).astype(BF16)
        dst[2, :rows] = jnp.where(low, rolled, 0.0).astype(BF16)
        dst[3, :rows] = jnp.where(low, 0.0, src).astype(BF16)


def _stage_sink_fill(sinks_ref, fill_ref, rows_per_slab, n_keys):
    shape = (KV_SLABS * rows_per_slab, LANES)
    row = lax.broadcasted_iota(jnp.int32, shape, 0)
    lane = lax.broadcasted_iota(jnp.int32, shape, 1)
    for kh in range(N_KV_HEADS):
        for j in range(SLAB_HEADS):
            head = lambda p: kh * Q_PER_KV + SLAB_HEADS * p + j
            snk = jnp.full(shape, sinks_ref[head(KV_SLABS - 1)], F32)
            for p in range(KV_SLABS - 2, -1, -1):
                snk = jnp.where(row < (p + 1) * rows_per_slab, sinks_ref[head(p)], snk)
            fill_ref[kh * SLAB_HEADS + j, :shape[0]] = jnp.where(lane == n_keys - LANES, snk, NEG)


def _attend(groups, n_keys):
    scores = [_dot_nt(g[0], g[1]) for g in groups]
    probs, rdens = [], []
    for s, (_, _, _, fill, mask_lo, mask_hi) in zip(scores, groups):
        s_lo, s_hi = s[:, :LANES], s[:, LANES:]
        if mask_lo is not None:
            s_lo = jnp.where(mask_lo, s_lo, NEG)
        s_hi = jnp.where(mask_hi, s_hi, fill)
        m = jnp.max(jnp.maximum(s_lo, s_hi), axis=1, keepdims=True)
        e_lo = jnp.exp(s_lo - m)
        e_hi = jnp.exp(s_hi - m)
        rdens.append(1.0 / jnp.sum(e_lo + e_hi, axis=1, keepdims=True))
        probs.append(jnp.concatenate([e_lo, e_hi], axis=1).astype(BF16))
    return [_dot(p[:, :n_keys], g[2]) * r for p, r, g in zip(probs, rdens, groups)]


def _slab(kh, p):
    return slice((kh * KV_SLABS + p) * LANES, (kh * KV_SLABS + p + 1) * LANES)


def _attend_rows(q_ref, ob_ref, rows, t, kst_ref, vst_ref, fill_ref, key0, n_keys, mask_lo, mask_hi):
    groups = []
    for kh in range(N_KV_HEADS):
        lhs = jnp.concatenate([q_ref[rows, _slab(kh, p)] for p in range(KV_SLABS)], axis=0)
        for j in range(SLAB_HEADS):
            g = kh * SLAB_HEADS + j
            groups.append((lhs, kst_ref[g, key0:key0 + KEY_TILE, :], vst_ref[g, key0:key0 + n_keys, :],
                           fill_ref[g, :KV_SLABS * t], mask_lo, mask_hi))
    outs = _attend(groups, n_keys)
    for kh in range(N_KV_HEADS):
        o = outs[kh * SLAB_HEADS]
        for j in range(1, SLAB_HEADS):
            o = o + outs[kh * SLAB_HEADS + j]
        for p in range(KV_SLABS):
            ob_ref[rows, _slab(kh, p)] = o[p * t:(p + 1) * t].astype(BF16)


def _merge(x, oa, ob, ga, gb, wpa_ref, wpb_ref, wout_ref):
    ya = _dot(oa, wpa_ref[...])
    yb = _dot(ob, wpb_ref[...])
    m = (ga.astype(F32) * ya + gb.astype(F32) * yb).astype(BF16)
    return x + _dot(m, wout_ref[...])


def _project_piece(piece, n, h_ref, win_ref, u_ref, vg_ref, ga_ref, gb_ref):
    per = GM_WIDTH // PROJ_TILE
    which, part = divmod(piece, per)
    cols = slice(part * PROJ_TILE, (part + 1) * PROJ_TILE)
    base = (C_U, C_V, C_GA, C_GB)[which]
    y = _dot(h_ref[:n], win_ref[:, base + part * PROJ_TILE:base + (part + 1) * PROJ_TILE])
    if which == 0:
        u_ref[:n, cols] = jax.nn.gelu(y).astype(BF16)
    elif which == 1:
        vg_ref[:n, cols] = jax.nn.gelu(y)
    elif which == 2:
        ga_ref[:n, cols] = jax.nn.sigmoid(y).astype(BF16)
    else:
        gb_ref[:n, cols] = jax.nn.sigmoid(y).astype(BF16)


N_PIECES = 4 * (GM_WIDTH // PROJ_TILE)


def _mix_prompt_step(ti, sinks_ref, x_ref, nmix_ref, win_ref, gmn_ref, ws_ref, bst_ref,
                     wpa_ref, wpb_ref, wout_ref, o_ref, kvt_ref,
                     h_ref, u_ref, vg_ref, v_ref, q_ref, ga_ref, gb_ref, carry_ref,
                     kst_ref, vst_ref, fill_ref, oa_ref, ob_ref):
    tm = TM_MIX
    x = x_ref[...]
    h_ref[...] = _rms(x, nmix_ref[...]).astype(BF16)
    q_ref[...] = (_dot(h_ref[...], win_ref[:, C_Q:C_KV]) * SCALE).astype(BF16)
    kv = _dot(h_ref[...], win_ref[:, C_KV:C_GA])
    _stage_kv(jnp.concatenate([carry_ref[...], kv], axis=0), kst_ref, vst_ref)
    carry_ref[...] = kv[tm - WINDOW:]
    kvt_ref[0] = kv[tm - WINDOW:]
    _stage_sink_fill(sinks_ref, fill_ref, CHUNK, BAND)

    lane = lax.broadcasted_iota(jnp.int32, (1, LANES), 1)
    mask_hi = lane < BAND - LANES
    n_chunks = tm // CHUNK
    for c in range(n_chunks):
        r0 = c * CHUNK
        mask_lo = (ti * tm + r0 - WINDOW + lane >= 0) if r0 < WINDOW else None
        _attend_rows(q_ref, ob_ref, slice(r0, r0 + CHUNK), CHUNK, kst_ref, vst_ref, fill_ref,
                     r0, BAND, mask_lo, mask_hi)
    for piece in range(N_PIECES):
        _project_piece(piece, tm, h_ref, win_ref, u_ref, vg_ref, ga_ref, gb_ref)
    v_ref[...] = _rms(vg_ref[...], gmn_ref[...]).astype(BF16)

    n_gm = tm // GM_CHUNK
    ri = lax.broadcasted_iota(jnp.int32, (GM_CHUNK, GM_CHUNK), 0)
    ci = lax.broadcasted_iota(jnp.int32, (GM_CHUNK, GM_CHUNK), 1)
    gm_mask = _chunk_causal(ri, ci).astype(F32)
    for g in range(GM_GROUPS):
        cols = slice(g * GM_GROUP_DIM, (g + 1) * GM_GROUP_DIM)
        w = (ws_ref[g] * gm_mask).astype(BF16)
        rhs = jnp.concatenate(
            [v_ref[c * GM_CHUNK:(c + 1) * GM_CHUNK, cols] for c in range(n_gm)], axis=1)
        sg = _dot(w, rhs) + bst_ref[:, g:g + 1]
        for c in range(n_gm):
            rows = slice(c * GM_CHUNK, (c + 1) * GM_CHUNK)
            oa_ref[rows, cols] = (u_ref[rows, cols].astype(F32)
                                  * sg[:, c * GM_CHUNK:(c + 1) * GM_CHUNK]).astype(BF16)

    o_ref[...] = _merge(x, oa_ref[...], ob_ref[...], ga_ref[...], gb_ref[...],
                        wpa_ref, wpb_ref, wout_ref)


def _mix_sample_step(sinks_ref, xs_ref, ck_ref, cv_ref, nmix_ref, win_ref, gmn_ref, wst_ref, bsts_ref,
                     wpa_ref, wpb_ref, wout_ref, os_ref, kvs_ref, vns_ref,
                     h_ref, u_ref, vg_ref, q_ref, ga_ref, gb_ref,
                     kst_ref, vst_ref, fill_ref, oa_ref, ob_ref, *, nb, s, wc):
    n = nb * s
    x = xs_ref[...]
    h_ref[:n] = _rms(x, nmix_ref[...]).astype(BF16)
    q_ref[:n] = (_dot(h_ref[:n], win_ref[:, C_Q:C_KV]) * SCALE).astype(BF16)
    kv_new = _dot(h_ref[:n], win_ref[:, C_KV:C_GA])
    kvs_ref[...] = kv_new
    for piece in range(N_PIECES):
        _project_piece(piece, n, h_ref, win_ref, u_ref, vg_ref, ga_ref, gb_ref)
    v_n = _rms(vg_ref[:n], gmn_ref[...])
    vns_ref[...] = v_n

    ri = lax.broadcasted_iota(jnp.int32, (n, n), 0)
    ci = lax.broadcasted_iota(jnp.int32, (n, n), 1)
    gm_mask = ((_div(ri, s) == _div(ci, s)) & _chunk_causal(_mod(ri, s), _mod(ci, s))).astype(F32)
    for g in range(GM_GROUPS):
        cols = slice(g * GM_GROUP_DIM, (g + 1) * GM_GROUP_DIM)
        w = (wst_ref[g] * gm_mask).astype(BF16)
        sg = _dot(w, v_n[:, cols].astype(BF16)) + bsts_ref[:, g:g + 1]
        oa_ref[:n, cols] = (u_ref[:n, cols].astype(F32) * sg).astype(BF16)

    n_keys = wc + s
    _stage_sink_fill(sinks_ref, fill_ref, s, n_keys)
    rows = KV_SLABS * s
    q_pos = PAST_LEN + _mod(lax.broadcasted_iota(jnp.int32, (rows, KEY_TILE), 0), s)
    col = lax.broadcasted_iota(jnp.int32, (rows, KEY_TILE), 1)
    dist = _div(q_pos, CHUNK) - _div(PAST_LEN - wc + col, CHUNK)
    valid = (dist >= 0) & (dist <= WINDOW_CHUNKS) & (col < n_keys)
    kst_ref[:, n_keys:KEY_TILE, :] = jnp.zeros((N_GROUPS, KEY_TILE - n_keys, LANES), BF16)
    for b in range(nb):
        r = slice(b * s, (b + 1) * s)
        new = kv_new[b * s:(b + 1) * s]
        kv = jnp.concatenate(
            [jnp.concatenate([ck_ref[b], new[:, :KVW]], axis=0),
             jnp.concatenate([cv_ref[b], new[:, KVW:]], axis=0)], axis=1)
        _stage_kv(kv, kst_ref, vst_ref)
        _attend_rows(q_ref, ob_ref, r, s, kst_ref, vst_ref, fill_ref, 0, n_keys,
                     valid[:, :LANES], valid[:, LANES:])

    os_ref[...] = _merge(x, oa_ref[:n], ob_ref[:n], ga_ref[:n], gb_ref[:n], wpa_ref, wpb_ref, wout_ref)


def _mix_body(sinks_ref, x_ref, xs_ref, ck_ref, cv_ref, nmix_ref, win_ref, gmn_ref, ws_ref, bst_ref,
              wst_ref, bsts_ref, wpa_ref, wpb_ref, wout_ref,
              o_ref, kvt_ref, os_ref, kvs_ref, vns_ref,
              h_ref, u_ref, vg_ref, v_ref, q_ref, ga_ref, gb_ref, carry_ref,
              kst_ref, vst_ref, fill_ref, oa_ref, ob_ref, *, n_blocks, blocks_per_stream, nb, s, wc):
    i = pl.program_id(0)

    @pl.when(i == 0)
    def _():
        carry_ref[...] = jnp.zeros(carry_ref.shape, F32)
        kst_ref[:, TM_MIX + WINDOW:, :] = jnp.zeros(
            (N_GROUPS, kst_ref.shape[1] - TM_MIX - WINDOW, LANES), BF16)

    @pl.when(i < n_blocks)
    def _():
        _mix_prompt_step(lax.rem(i, blocks_per_stream), sinks_ref, x_ref, nmix_ref, win_ref, gmn_ref,
                         ws_ref, bst_ref, wpa_ref, wpb_ref, wout_ref, o_ref, kvt_ref,
                         h_ref, u_ref, vg_ref, v_ref, q_ref, ga_ref, gb_ref, carry_ref,
                         kst_ref, vst_ref, fill_ref, oa_ref, ob_ref)

    @pl.when(i == n_blocks)
    def _():
        _mix_sample_step(sinks_ref, xs_ref, ck_ref, cv_ref, nmix_ref, win_ref, gmn_ref, wst_ref,
                         bsts_ref, wpa_ref, wpb_ref, wout_ref, os_ref, kvs_ref, vns_ref,
                         h_ref, u_ref, vg_ref, q_ref, ga_ref, gb_ref,
                         kst_ref, vst_ref, fill_ref, oa_ref, ob_ref, nb=nb, s=s, wc=wc)


def _mix(sinks, x, xs, ck, cv, nmix, win, gmn, ws, bst, wst, bsts, wpa, wpb, wout, b, nb, s):
    n, ns = x.shape[0], xs.shape[0]
    tm = TM_MIX
    t = n // b
    wc = ck.shape[1]
    n_blocks = n // tm
    blocks_per_stream = t // tm
    last = n_blocks - 1
    row = pl.BlockSpec((tm, D_MODEL), lambda i, sk: (jnp.minimum(i, last), 0))
    tail = pl.BlockSpec((1, WINDOW, 2 * KVW),
                        lambda i, sk: (jnp.minimum(i, last) // blocks_per_stream, 0, 0))
    full = lambda shape: pl.BlockSpec(shape, lambda i, sk: (0,) * len(shape))
    consts = (xs, ck, cv, nmix, win, gmn, ws, bst, wst, bsts, wpa, wpb, wout)
    grid_spec = pltpu.PrefetchScalarGridSpec(
        num_scalar_prefetch=1,
        grid=(n_blocks + 1,),
        in_specs=[row] + [_const_spec(a.shape) for a in consts],
        out_specs=[row, tail, full((ns, D_MODEL)), full((ns, 2 * KVW)), full((ns, GM_WIDTH))],
        scratch_shapes=[pltpu.VMEM((tm, D_MODEL), BF16),
                        pltpu.VMEM((tm, GM_WIDTH), BF16),
                        pltpu.VMEM((tm, GM_WIDTH), F32),
                        pltpu.VMEM((tm, GM_WIDTH), BF16),
                        pltpu.VMEM((tm, QW), BF16),
                        pltpu.VMEM((tm, D_MODEL), BF16),
                        pltpu.VMEM((tm, D_MODEL), BF16),
                        pltpu.VMEM((WINDOW, 2 * KVW), F32),
                        pltpu.VMEM((N_GROUPS, tm - CHUNK + KEY_TILE, LANES), BF16),
                        pltpu.VMEM((N_GROUPS, tm + WINDOW, LANES), BF16),
                        pltpu.VMEM((N_GROUPS, KV_SLABS * CHUNK, LANES), F32),
                        pltpu.VMEM((tm, GM_WIDTH), BF16),
                        pltpu.VMEM((tm, QW), BF16)],
    )
    return pl.pallas_call(
        functools.partial(_mix_body, n_blocks=n_blocks, blocks_per_stream=blocks_per_stream,
                          nb=nb, s=s, wc=wc),
        grid_spec=grid_spec,
        out_shape=[jax.ShapeDtypeStruct((n, D_MODEL), F32),
                   jax.ShapeDtypeStruct((b, WINDOW, 2 * KVW), F32),
                   jax.ShapeDtypeStruct((ns, D_MODEL), F32),
                   jax.ShapeDtypeStruct((ns, 2 * KVW), F32),
                   jax.ShapeDtypeStruct((ns, GM_WIDTH), F32)],
        compiler_params=_params(),
        name="mix",
    )(sinks, x, *consts)


def kernel(x_prompt, x_sample, cache_k, cache_v, norm_ffn1, ffn1_w1, ffn1_w3, ffn1_w2, norm_mix, w_in, gm_norm, gm_ws, gm_bs, sinks, w_pa, w_pb, w_out, norm_ffn2, ffn2_w1, ffn2_w3, ffn2_w2, norm_final):
    depth = norm_ffn1.shape[0]
    b, t, _ = x_prompt.shape
    nb, s, _ = x_sample.shape
    wc = cache_k.shape[2]
    assert t % TM_MIX == 0 and (b * t) % TM_FFN == 0 and t >= WINDOW and TM_MIX >= WINDOW
    assert wc == WINDOW and wc + s < KEY_TILE and (nb * s) % 8 == 0 and nb * s <= TM_MIX
    row = lambda a: a.reshape(1, -1).astype(F32)
    gf = row(norm_final)

    xp = x_prompt.reshape(b * t, D_MODEL)
    xs = x_sample.reshape(nb * s, D_MODEL)
    kp_l, vp_l, ks_l, vs_l, gs_l = [], [], [], [], []
    for l in range(depth):
        last = l == depth - 1
        ws = gm_ws[l].astype(F32)
        bs = gm_bs[l].astype(F32)
        xp, xs = _ffn(xp, xs, row(norm_ffn1[l]), ffn1_w1[l].astype(BF16), ffn1_w3[l].astype(BF16),
                      ffn1_w2[l].astype(BF16), None)
        xp, kv_tail, xs, kv_s, vn_s = _mix(
            sinks[l].astype(F32), xp, xs,
            cache_k[l].reshape(nb, wc, KVW), cache_v[l].reshape(nb, wc, KVW),
            row(norm_mix[l]), w_in[l].astype(BF16), row(gm_norm[l]),
            ws, bs.T, jnp.tile(ws[:, :s, :s], (1, nb, nb)), jnp.tile(bs[:, :s].T, (nb, 1)),
            w_pa[l].astype(BF16), w_pb[l].astype(BF16), w_out[l].astype(BF16), b, nb, s)
        xp, xs = _ffn(xp, xs, row(norm_ffn2[l]), ffn2_w1[l].astype(BF16), ffn2_w3[l].astype(BF16),
                      ffn2_w2[l].astype(BF16), gf if last else None)
        kp_l.append(kv_tail[:, :, :KVW].reshape(b, WINDOW, N_KV_HEADS, HEAD_DIM))
        vp_l.append(kv_tail[:, :, KVW:].reshape(b, WINDOW, N_KV_HEADS, HEAD_DIM))
        ks_l.append(kv_s[:, :KVW].reshape(nb, s, N_KV_HEADS, HEAD_DIM))
        vs_l.append(kv_s[:, KVW:].reshape(nb, s, N_KV_HEADS, HEAD_DIM))
        gs_l.append(vn_s.reshape(nb, s, GM_WIDTH))

    return (xp.reshape(b, t, D_MODEL), xs.reshape(nb, s, D_MODEL), jnp.stack(kp_l), jnp.stack(vp_l),
            jnp.stack(ks_l), jnp.stack(vs_l), jnp.stack(gs_l))
```

```python
import functools

import jax
import jax.numpy as jnp
from jax import lax
from jax.experimental import pallas as pl
from jax.experimental.pallas import tpu as pltpu

D_MODEL = 1024
PAST_LEN = 1024
CHUNK = 64
GM_CHUNK = 128
GM_GROUPS = 8
GM_WIDTH = 1024
GM_GROUP_DIM = GM_WIDTH // GM_GROUPS
N_HEADS = 16
N_KV_HEADS = 2
HEAD_DIM = 64
Q_PER_KV = N_HEADS // N_KV_HEADS
WINDOW = 128
WINDOW_CHUNKS = WINDOW // CHUNK
BAND = WINDOW + CHUNK
QW = N_HEADS * HEAD_DIM
KVW = N_KV_HEADS * HEAD_DIM
D_FF = 2816
EPS = 1e-6
NEG = -1e30
SCALE = HEAD_DIM ** -0.5

C_U = 0
C_V = GM_WIDTH
C_Q = 2 * GM_WIDTH
C_KV = C_Q + QW
C_GA = C_KV + 2 * KVW
C_GB = C_GA + D_MODEL
IN_COLS = C_GB + D_MODEL

LANES = 128
SLAB_HEADS = LANES // HEAD_DIM
KV_SLABS = Q_PER_KV // SLAB_HEADS
N_GROUPS = N_KV_HEADS * SLAB_HEADS
KEY_TILE = 2 * LANES
FF_TILE = 256
VMEM_LIMIT = 56 * 1024 * 1024
TM_FFN = 1024
TM_MIX = 512

F32 = jnp.float32
BF16 = jnp.bfloat16


def _rms(x, g):
    return x * lax.rsqrt(jnp.mean(x * x, axis=-1, keepdims=True) + EPS) * g


def _dot(a, b):
    return jnp.dot(a, b, preferred_element_type=F32)


def _dot_nt(a, b):
    return lax.dot_general(a, b, (((1,), (1,)), ((), ())), preferred_element_type=F32)


def _const_spec(shape):
    zeros = (0,) * len(shape)
    return pl.BlockSpec(shape, lambda *_: zeros, pipeline_mode=pl.Buffered(1))


def _params():
    return pltpu.CompilerParams(dimension_semantics=("arbitrary",), vmem_limit_bytes=VMEM_LIMIT)


def _div(x, d):
    assert d & (d - 1) == 0
    return lax.shift_right_logical(x, d.bit_length() - 1)


def _mod(x, d):
    assert d & (d - 1) == 0
    return x & (d - 1)


def _chunk_causal(i, j):
    return _div(i, CHUNK) >= _div(j, CHUNK)


def _ffn_rows(x, g_ref, w1_ref, w3_ref, w2_ref, gf_ref):
    h = _rms(x, g_ref[...]).astype(BF16)
    acc = jnp.zeros(x.shape, F32)
    for f in range(D_FF // FF_TILE):
        sl = slice(f * FF_TILE, (f + 1) * FF_TILE)
        a = _dot(h, w1_ref[:, sl])
        b = _dot(h, w3_ref[:, sl])
        gated = (a * jax.nn.sigmoid(a) * b).astype(BF16)
        acc = acc + _dot(gated, w2_ref[sl, :])
    y = x + 0.5 * acc
    return y if gf_ref is None else _rms(y, gf_ref[...])


def _ffn_body(x_ref, xs_ref, g_ref, w1_ref, w3_ref, w2_ref, *rest, final, n_blocks):
    gf_ref = rest[0] if final else None
    o_ref, os_ref = rest[-2:]
    i = pl.program_id(0)

    @pl.when(i < n_blocks)
    def _():
        o_ref[...] = _ffn_rows(x_ref[...], g_ref, w1_ref, w3_ref, w2_ref, gf_ref)

    @pl.when(i == n_blocks)
    def _():
        os_ref[...] = _ffn_rows(xs_ref[...], g_ref, w1_ref, w3_ref, w2_ref, gf_ref)


def _ffn(x, xs, g, w1, w3, w2, gf):
    n, ns = x.shape[0], xs.shape[0]
    n_blocks = n // TM_FFN
    final = gf is not None
    row = pl.BlockSpec((TM_FFN, D_MODEL), lambda i: (jnp.minimum(i, n_blocks - 1), 0))
    in_specs = [row, _const_spec((ns, D_MODEL)), _const_spec((1, D_MODEL)),
                _const_spec((D_MODEL, D_FF)), _const_spec((D_MODEL, D_FF)),
                _const_spec((D_FF, D_MODEL))]
    args = [x, xs, g, w1, w3, w2]
    if final:
        in_specs.append(_const_spec((1, D_MODEL)))
        args.append(gf)
    return pl.pallas_call(
        functools.partial(_ffn_body, final=final, n_blocks=n_blocks),
        grid=(n_blocks + 1,),
        in_specs=in_specs,
        out_specs=[row, pl.BlockSpec((ns, D_MODEL), lambda i: (0, 0))],
        out_shape=[jax.ShapeDtypeStruct((n, D_MODEL), F32), jax.ShapeDtypeStruct((ns, D_MODEL), F32)],
        compiler_params=_params(),
        name="ffn_final" if final else "ffn",
    )(*args)


def _project(h_ref, n, win_ref, gmn_ref, u_ref, v_ref, q_ref, kv_ref, ga_ref, gb_ref):
    def proj(lo, hi):
        return _dot(h_ref[:n], win_ref[:, lo:hi])

    u_ref[:n] = jax.nn.gelu(proj(C_U, C_V)).astype(u_ref.dtype)
    v_ref[:n] = _rms(jax.nn.gelu(proj(C_V, C_Q)), gmn_ref[...]).astype(v_ref.dtype)
    ga_ref[:n] = jax.nn.sigmoid(proj(C_GA, C_GB)).astype(ga_ref.dtype)
    gb_ref[:n] = jax.nn.sigmoid(proj(C_GB, IN_COLS)).astype(gb_ref.dtype)
    q_ref[:n] = (proj(C_Q, C_KV) * SCALE).astype(q_ref.dtype)
    kv_ref[:n] = proj(C_KV, C_GA)


def _inproj_body(x_ref, g_ref, w_ref, gm_ref, u_ref, v_ref, q_ref, kv_ref, ga_ref, gb_ref, h_ref):
    h_ref[...] = _rms(x_ref[...], g_ref[...]).astype(BF16)
    _project(h_ref, TM_MIX, w_ref, gm_ref, u_ref, v_ref, q_ref, kv_ref, ga_ref, gb_ref)


def _inproj(x, g, w_in, gm):
    n = x.shape[0]
    tm = TM_MIX
    row = lambda w: pl.BlockSpec((tm, w), lambda i: (i, 0))
    wide = jax.ShapeDtypeStruct((n, D_MODEL), BF16)
    return pl.pallas_call(
        _inproj_body,
        grid=(n // tm,),
        in_specs=[row(D_MODEL), _const_spec((1, D_MODEL)), _const_spec((D_MODEL, IN_COLS)),
                  _const_spec((1, GM_WIDTH))],
        out_specs=[row(GM_WIDTH), row(GM_WIDTH), row(QW), row(2 * KVW), row(D_MODEL), row(D_MODEL)],
        out_shape=[wide, wide, wide, jax.ShapeDtypeStruct((n, 2 * KVW), F32), wide, wide],
        scratch_shapes=[pltpu.VMEM((tm, D_MODEL), BF16)],
        compiler_params=_params(),
        name="inproj",
    )(x, g, w_in, gm)


def _stage_kv(kv, kst_ref, vst_ref):
    rows = kv.shape[0]
    k2 = kv[:, :KVW]
    v2 = kv[:, KVW:]
    low = lax.broadcasted_iota(jnp.int32, k2.shape, 1) < HEAD_DIM
    for src, dst in ((k2, kst_ref), (v2, vst_ref)):
        rolled = pltpu.roll(src, HEAD_DIM, 1)
        dst[0, :rows] = jnp.where(low, src, 0.0).astype(BF16)
        dst[1, :rows] = jnp.where(low, 0.0, rolled).astype(BF16)
        dst[2, :rows] = jnp.where(low, rolled, 0.0).astype(BF16)
        dst[3, :rows] = jnp.where(low, 0.0, src).astype(BF16)
    pad = kst_ref.shape[1] - rows
    kst_ref[:, rows:, :] = jnp.zeros((N_GROUPS, pad, LANES), BF16)


def _stage_sink_fill(sinks_ref, fill_ref, rows_per_slab, n_keys):
    shape = fill_ref.shape[1:]
    row = lax.broadcasted_iota(jnp.int32, shape, 0)
    lane = lax.broadcasted_iota(jnp.int32, shape, 1)
    for kh in range(N_KV_HEADS):
        for j in range(SLAB_HEADS):
            head = lambda p: kh * Q_PER_KV + SLAB_HEADS * p + j
            snk = jnp.full(shape, sinks_ref[head(KV_SLABS - 1)], F32)
            for p in range(KV_SLABS - 2, -1, -1):
                snk = jnp.where(row < (p + 1) * rows_per_slab, sinks_ref[head(p)], snk)
            fill_ref[kh * SLAB_HEADS + j] = jnp.where(lane == n_keys - LANES, snk, NEG)


def _attend(groups, n_keys):
    scores = [_dot_nt(g[0], g[1]) for g in groups]
    probs, rdens = [], []
    for s, (_, _, _, fill, mask_lo, mask_hi) in zip(scores, groups):
        s_lo, s_hi = s[:, :LANES], s[:, LANES:]
        if mask_lo is not None:
            s_lo = jnp.where(mask_lo, s_lo, NEG)
        s_hi = jnp.where(mask_hi, s_hi, fill)
        m = jnp.max(jnp.maximum(s_lo, s_hi), axis=1, keepdims=True)
        e_lo = jnp.exp(s_lo - m)
        e_hi = jnp.exp(s_hi - m)
        rdens.append(1.0 / jnp.sum(e_lo + e_hi, axis=1, keepdims=True))
        probs.append(jnp.concatenate([e_lo, e_hi], axis=1).astype(BF16))
    return [_dot(p[:, :n_keys], g[2]) * r for p, r, g in zip(probs, rdens, groups)]


def _slab(kh, p):
    return slice((kh * KV_SLABS + p) * LANES, (kh * KV_SLABS + p + 1) * LANES)


def _attend_rows(q_ref, ob_ref, rows, t, kst_ref, vst_ref, fill_ref, key0, n_keys, mask_lo, mask_hi):
    groups = []
    for kh in range(N_KV_HEADS):
        lhs = jnp.concatenate([q_ref[rows, _slab(kh, p)] for p in range(KV_SLABS)], axis=0)
        for j in range(SLAB_HEADS):
            g = kh * SLAB_HEADS + j
            groups.append((lhs, kst_ref[g, key0:key0 + KEY_TILE, :], vst_ref[g, key0:key0 + n_keys, :],
                           fill_ref[g], mask_lo, mask_hi))
    outs = _attend(groups, n_keys)
    for kh in range(N_KV_HEADS):
        o = outs[kh * SLAB_HEADS]
        for j in range(1, SLAB_HEADS):
            o = o + outs[kh * SLAB_HEADS + j]
        for p in range(KV_SLABS):
            ob_ref[rows, _slab(kh, p)] = o[p * t:(p + 1) * t].astype(BF16)


def _merge(x, oa, ob, ga, gb, wpa_ref, wpb_ref, wout_ref):
    ya = _dot(oa, wpa_ref[...])
    yb = _dot(ob, wpb_ref[...])
    m = (ga.astype(F32) * ya + gb.astype(F32) * yb).astype(BF16)
    return x + _dot(m, wout_ref[...])


def _mix_prompt_body(sinks_ref, x_ref, u_ref, v_ref, q_ref, kvp_ref, kvc_ref, ga_ref, gb_ref,
                     ws_ref, bst_ref, wpa_ref, wpb_ref, wout_ref, o_ref,
                     kst_ref, vst_ref, fill_ref, oa_ref, ob_ref, *, blocks_per_stream):
    tm = TM_MIX
    ti = lax.rem(pl.program_id(0), blocks_per_stream)

    n_gm = tm // GM_CHUNK
    ri = lax.broadcasted_iota(jnp.int32, (GM_CHUNK, GM_CHUNK), 0)
    ci = lax.broadcasted_iota(jnp.int32, (GM_CHUNK, GM_CHUNK), 1)
    gm_mask = _chunk_causal(ri, ci).astype(F32)
    for g in range(GM_GROUPS):
        cols = slice(g * GM_GROUP_DIM, (g + 1) * GM_GROUP_DIM)
        w = (ws_ref[g] * gm_mask).astype(BF16)
        rhs = jnp.concatenate(
            [v_ref[c * GM_CHUNK:(c + 1) * GM_CHUNK, cols] for c in range(n_gm)], axis=1)
        sg = _dot(w, rhs) + bst_ref[:, g:g + 1]
        for c in range(n_gm):
            rows = slice(c * GM_CHUNK, (c + 1) * GM_CHUNK)
            oa_ref[rows, cols] = (u_ref[rows, cols].astype(F32)
                                  * sg[:, c * GM_CHUNK:(c + 1) * GM_CHUNK]).astype(BF16)

    _stage_kv(jnp.concatenate([kvp_ref[...], kvc_ref[...]], axis=0), kst_ref, vst_ref)
    _stage_sink_fill(sinks_ref, fill_ref, CHUNK, BAND)
    lane = lax.broadcasted_iota(jnp.int32, (1, LANES), 1)
    mask_hi = lane < BAND - LANES
    for c in range(tm // CHUNK):
        r0 = c * CHUNK
        mask_lo = (ti * tm + r0 - WINDOW + lane >= 0) if r0 < WINDOW else None
        _attend_rows(q_ref, ob_ref, slice(r0, r0 + CHUNK), CHUNK, kst_ref, vst_ref, fill_ref,
                     r0, BAND, mask_lo, mask_hi)

    o_ref[...] = _merge(x_ref[...], oa_ref[...], ob_ref[...], ga_ref[...], gb_ref[...],
                        wpa_ref, wpb_ref, wout_ref)


def _mix_prompt(sinks, x, u, v, q, kv, ga, gb, ws, bst, wpa, wpb, wout, blocks_per_stream):
    n = x.shape[0]
    tm = TM_MIX
    blk = lambda w: pl.BlockSpec((tm, w), lambda i, s: (i, 0))
    prev = pl.BlockSpec((WINDOW, 2 * KVW), lambda i, s: (jnp.maximum(i * (tm // WINDOW) - 1, 0), 0))
    grid_spec = pltpu.PrefetchScalarGridSpec(
        num_scalar_prefetch=1,
        grid=(n // tm,),
        in_specs=[blk(D_MODEL), blk(GM_WIDTH), blk(GM_WIDTH), blk(QW), prev, blk(2 * KVW),
                  blk(D_MODEL), blk(D_MODEL),
                  _const_spec((GM_GROUPS, GM_CHUNK, GM_CHUNK)), _const_spec((GM_CHUNK, GM_GROUPS)),
                  _const_spec((GM_WIDTH, D_MODEL)), _const_spec((QW, D_MODEL)),
                  _const_spec((D_MODEL, D_MODEL))],
        out_specs=blk(D_MODEL),
        scratch_shapes=[pltpu.VMEM((N_GROUPS, tm - CHUNK + KEY_TILE, LANES), BF16),
                        pltpu.VMEM((N_GROUPS, tm + WINDOW, LANES), BF16),
                        pltpu.VMEM((N_GROUPS, KV_SLABS * CHUNK, LANES), F32),
                        pltpu.VMEM((tm, GM_WIDTH), BF16),
                        pltpu.VMEM((tm, QW), BF16)],
    )
    return pl.pallas_call(
        functools.partial(_mix_prompt_body, blocks_per_stream=blocks_per_stream),
        grid_spec=grid_spec,
        out_shape=jax.ShapeDtypeStruct((n, D_MODEL), F32),
        compiler_params=_params(),
        name="mix_prompt",
    )(sinks, x, u, v, q, kv, kv, ga, gb, ws, bst, wpa, wpb, wout)


def _mix_sample_body(sinks_ref, xs_ref, ck_ref, cv_ref, nmix_ref, win_ref, gmn_ref, wst_ref, bsts_ref,
                     wpa_ref, wpb_ref, wout_ref, os_ref, kvs_ref, vns_ref,
                     h_ref, u_ref, q_ref, ga_ref, gb_ref,
                     kst_ref, vst_ref, fill_ref, oa_ref, ob_ref, *, nb, s, wc):
    n = nb * s
    x = xs_ref[...]
    h_ref[...] = _rms(x, nmix_ref[...]).astype(BF16)
    _project(h_ref, n, win_ref, gmn_ref, u_ref, vns_ref, q_ref, kvs_ref, ga_ref, gb_ref)

    ri = lax.broadcasted_iota(jnp.int32, (n, n), 0)
    ci = lax.broadcasted_iota(jnp.int32, (n, n), 1)
    gm_mask = ((_div(ri, s) == _div(ci, s)) & _chunk_causal(_mod(ri, s), _mod(ci, s))).astype(F32)
    for g in range(GM_GROUPS):
        cols = slice(g * GM_GROUP_DIM, (g + 1) * GM_GROUP_DIM)
        w = (wst_ref[g] * gm_mask).astype(BF16)
        sg = _dot(w, vns_ref[:, cols].astype(BF16)) + bsts_ref[:, g:g + 1]
        oa_ref[:, cols] = (u_ref[:, cols].astype(F32) * sg).astype(BF16)

    n_keys = wc + s
    _stage_sink_fill(sinks_ref, fill_ref, s, n_keys)
    rows = KV_SLABS * s
    q_pos = PAST_LEN + _mod(lax.broadcasted_iota(jnp.int32, (rows, KEY_TILE), 0), s)
    col = lax.broadcasted_iota(jnp.int32, (rows, KEY_TILE), 1)
    dist = _div(q_pos, CHUNK) - _div(PAST_LEN - wc + col, CHUNK)
    valid = (dist >= 0) & (dist <= WINDOW_CHUNKS) & (col < n_keys)
    for b in range(nb):
        r = slice(b * s, (b + 1) * s)
        new = kvs_ref[r, :]
        kv = jnp.concatenate(
            [jnp.concatenate([ck_ref[b], new[:, :KVW]], axis=0),
             jnp.concatenate([cv_ref[b], new[:, KVW:]], axis=0)], axis=1)
        _stage_kv(kv, kst_ref, vst_ref)
        _attend_rows(q_ref, ob_ref, r, s, kst_ref, vst_ref, fill_ref, 0, n_keys,
                     valid[:, :LANES], valid[:, LANES:])

    os_ref[...] = _merge(x, oa_ref[...], ob_ref[...], ga_ref[...], gb_ref[...],
                         wpa_ref, wpb_ref, wout_ref)


def _mix_sample(sinks, xs, ck, cv, nmix, win, gmn, wst, bsts, wpa, wpb, wout, nb, s):
    ns = xs.shape[0]
    wc = ck.shape[1]
    full = lambda shape: pl.BlockSpec(shape, lambda i, sk: (0,) * len(shape))
    args = (xs, ck, cv, nmix, win, gmn, wst, bsts, wpa, wpb, wout)
    wide = pltpu.VMEM((ns, D_MODEL), BF16)
    grid_spec = pltpu.PrefetchScalarGridSpec(
        num_scalar_prefetch=1,
        grid=(1,),
        in_specs=[full(a.shape) for a in args],
        out_specs=[full((ns, D_MODEL)), full((ns, 2 * KVW)), full((ns, GM_WIDTH))],
        scratch_shapes=[wide, wide, wide, wide, wide,
                        pltpu.VMEM((N_GROUPS, KEY_TILE, LANES), BF16),
                        pltpu.VMEM((N_GROUPS, wc + s, LANES), BF16),
                        pltpu.VMEM((N_GROUPS, KV_SLABS * s, LANES), F32),
                        wide, wide],
    )
    return pl.pallas_call(
        functools.partial(_mix_sample_body, nb=nb, s=s, wc=wc),
        grid_spec=grid_spec,
        out_shape=[jax.ShapeDtypeStruct((ns, D_MODEL), F32),
                   jax.ShapeDtypeStruct((ns, 2 * KVW), F32),
                   jax.ShapeDtypeStruct((ns, GM_WIDTH), F32)],
        compiler_params=_params(),
        name="mix_sample",
    )(sinks, *args)


def kernel(x_prompt, x_sample, cache_k, cache_v, norm_ffn1, ffn1_w1, ffn1_w3, ffn1_w2, norm_mix, w_in, gm_norm, gm_ws, gm_bs, sinks, w_pa, w_pb, w_out, norm_ffn2, ffn2_w1, ffn2_w3, ffn2_w2, norm_final):
    depth = norm_ffn1.shape[0]
    b, t, _ = x_prompt.shape
    nb, s, _ = x_sample.shape
    wc = cache_k.shape[2]
    assert t % TM_MIX == 0 and (b * t) % TM_FFN == 0 and t >= WINDOW and TM_MIX % WINDOW == 0
    assert wc + s < KEY_TILE and wc % 16 == 0 and s % 16 == 0
    row = lambda a: a.reshape(1, -1).astype(F32)
    gf = row(norm_final)

    xp = x_prompt.reshape(b * t, D_MODEL)
    xs = x_sample.reshape(nb * s, D_MODEL)
    kp_l, vp_l, ks_l, vs_l, gs_l = [], [], [], [], []
    for l in range(depth):
        last = l == depth - 1
        ws = gm_ws[l].astype(F32)
        bs = gm_bs[l].astype(F32)
        snk = sinks[l].astype(F32)
        nmix, gmn, win = row(norm_mix[l]), row(gm_norm[l]), w_in[l].astype(BF16)
        wpa, wpb, wout = w_pa[l].astype(BF16), w_pb[l].astype(BF16), w_out[l].astype(BF16)

        xp, xs = _ffn(xp, xs, row(norm_ffn1[l]), ffn1_w1[l].astype(BF16), ffn1_w3[l].astype(BF16),
                      ffn1_w2[l].astype(BF16), None)
        u, v, q, kv, ga, gb = _inproj(xp, nmix, win, gmn)
        xp = _mix_prompt(snk, xp, u, v, q, kv, ga, gb, ws, bs.T, wpa, wpb, wout, t // TM_MIX)
        xs, kv_s, vn_s = _mix_sample(
            snk, xs, cache_k[l].reshape(nb, wc, KVW), cache_v[l].reshape(nb, wc, KVW), nmix, win, gmn,
            jnp.tile(ws[:, :s, :s], (1, nb, nb)), jnp.tile(bs[:, :s].T, (nb, 1)), wpa, wpb, wout, nb, s)
        xp, xs = _ffn(xp, xs, row(norm_ffn2[l]), ffn2_w1[l].astype(BF16), ffn2_w3[l].astype(BF16),
                      ffn2_w2[l].astype(BF16), gf if last else None)

        keep = min(WINDOW, t)
        kv_tail = kv.reshape(b, t, 2 * KVW)[:, t - keep:]
        kp_l.append(kv_tail[:, :, :KVW].reshape(b, keep, N_KV_HEADS, HEAD_DIM))
        vp_l.append(kv_tail[:, :, KVW:].reshape(b, keep, N_KV_HEADS, HEAD_DIM))
        ks_l.append(kv_s[:, :KVW].reshape(nb, s, N_KV_HEADS, HEAD_DIM))
        vs_l.append(kv_s[:, KVW:].reshape(nb, s, N_KV_HEADS, HEAD_DIM))
        gs_l.append(vn_s.reshape(nb, s, GM_WIDTH))

    return (xp.reshape(b, t, D_MODEL), xs.reshape(nb, s, D_MODEL), jnp.stack(kp_l), jnp.stack(vp_l),
            jnp.stack(ks_l), jnp.stack(vs_l), jnp.stack(gs_l))
```

```python
import functools

import jax
import jax.numpy as jnp
from jax import lax
from jax.experimental import pallas as pl
from jax.experimental.pallas import tpu as pltpu

D_MODEL = 1024
PAST_LEN = 1024
CHUNK = 64
GM_CHUNK = 128
GM_GROUPS = 8
GM_WIDTH = 1024
GM_GROUP_DIM = GM_WIDTH // GM_GROUPS
N_HEADS = 16
N_KV_HEADS = 2
HEAD_DIM = 64
Q_PER_KV = N_HEADS // N_KV_HEADS
WINDOW = 128
WINDOW_CHUNKS = WINDOW // CHUNK
BAND = WINDOW + CHUNK
QW = N_HEADS * HEAD_DIM
KVW = N_KV_HEADS * HEAD_DIM
D_FF = 2816
EPS = 1e-6
NEG = -1e30
SCALE = HEAD_DIM ** -0.5

C_U = 0
C_V = GM_WIDTH
C_Q = 2 * GM_WIDTH
C_KV = C_Q + QW
C_GA = C_KV + 2 * KVW
C_GB = C_GA + D_MODEL
IN_COLS = C_GB + D_MODEL

LANES = 128
SLAB_HEADS = LANES // HEAD_DIM
KV_SLABS = Q_PER_KV // SLAB_HEADS
N_GROUPS = N_KV_HEADS * SLAB_HEADS
KEY_TILE = 2 * LANES
FF_TILE = 256
VMEM_LIMIT = 56 * 1024 * 1024
TM_FFN = 1024
TM_MIX = 512
ATTN_BATCH = 16

F32 = jnp.float32
BF16 = jnp.bfloat16


def _rms(x, g):
    return x * lax.rsqrt(jnp.mean(x * x, axis=-1, keepdims=True) + EPS) * g


def _dot(a, b):
    return jnp.dot(a, b, preferred_element_type=F32)


def _dot_nt(a, b):
    return lax.dot_general(a, b, (((1,), (1,)), ((), ())), preferred_element_type=F32)


def _const_spec(shape):
    zeros = (0,) * len(shape)
    return pl.BlockSpec(shape, lambda *_: zeros, pipeline_mode=pl.Buffered(1))


def _params():
    return pltpu.CompilerParams(dimension_semantics=("arbitrary",), vmem_limit_bytes=VMEM_LIMIT)


def _div(x, d):
    assert d & (d - 1) == 0
    return lax.shift_right_logical(x, d.bit_length() - 1)


def _mod(x, d):
    assert d & (d - 1) == 0
    return x & (d - 1)


def _chunk_causal(i, j):
    return _div(i, CHUNK) >= _div(j, CHUNK)


def _ffn_rows(x, g_ref, w1_ref, w3_ref, w2_ref, gf_ref):
    h = _rms(x, g_ref[...]).astype(BF16)
    acc = jnp.zeros(x.shape, F32)
    for f in range(D_FF // FF_TILE):
        sl = slice(f * FF_TILE, (f + 1) * FF_TILE)
        a = _dot(h, w1_ref[:, sl])
        b = _dot(h, w3_ref[:, sl])
        gated = (a * jax.nn.sigmoid(a) * b).astype(BF16)
        acc = acc + _dot(gated, w2_ref[sl, :])
    y = x + 0.5 * acc
    return y if gf_ref is None else _rms(y, gf_ref[...])


def _ffn_body(x_ref, xs_ref, g_ref, w1_ref, w3_ref, w2_ref, *rest, final, n_blocks):
    gf_ref = rest[0] if final else None
    o_ref, os_ref = rest[-2:]
    i = pl.program_id(0)

    @pl.when(i < n_blocks)
    def _():
        o_ref[...] = _ffn_rows(x_ref[...], g_ref, w1_ref, w3_ref, w2_ref, gf_ref)

    @pl.when(i == n_blocks)
    def _():
        os_ref[...] = _ffn_rows(xs_ref[...], g_ref, w1_ref, w3_ref, w2_ref, gf_ref)


def _ffn(x, xs, g, w1, w3, w2, gf):
    n, ns = x.shape[0], xs.shape[0]
    n_blocks = n // TM_FFN
    final = gf is not None
    row = pl.BlockSpec((TM_FFN, D_MODEL), lambda i: (jnp.minimum(i, n_blocks - 1), 0))
    in_specs = [row, _const_spec((ns, D_MODEL)), _const_spec((1, D_MODEL)),
                _const_spec((D_MODEL, D_FF)), _const_spec((D_MODEL, D_FF)),
                _const_spec((D_FF, D_MODEL))]
    args = [x, xs, g, w1, w3, w2]
    if final:
        in_specs.append(_const_spec((1, D_MODEL)))
        args.append(gf)
    return pl.pallas_call(
        functools.partial(_ffn_body, final=final, n_blocks=n_blocks),
        grid=(n_blocks + 1,),
        in_specs=in_specs,
        out_specs=[row, pl.BlockSpec((ns, D_MODEL), lambda i: (0, 0))],
        out_shape=[jax.ShapeDtypeStruct((n, D_MODEL), F32), jax.ShapeDtypeStruct((ns, D_MODEL), F32)],
        compiler_params=_params(),
        name="ffn_final" if final else "ffn",
    )(*args)


def _project(h_ref, n, win_ref, gmn_ref, u_ref, v_ref, q_ref, kv_ref, ga_ref, gb_ref):
    def proj(lo, hi):
        return _dot(h_ref[:n], win_ref[:, lo:hi])

    u_ref[:n] = jax.nn.gelu(proj(C_U, C_V)).astype(u_ref.dtype)
    v_ref[:n] = _rms(jax.nn.gelu(proj(C_V, C_Q)), gmn_ref[...]).astype(v_ref.dtype)
    ga_ref[:n] = jax.nn.sigmoid(proj(C_GA, C_GB)).astype(ga_ref.dtype)
    gb_ref[:n] = jax.nn.sigmoid(proj(C_GB, IN_COLS)).astype(gb_ref.dtype)
    q_ref[:n] = (proj(C_Q, C_KV) * SCALE).astype(q_ref.dtype)
    kv_ref[:n] = proj(C_KV, C_GA)


def _inproj_body(x_ref, g_ref, w_ref, gm_ref, u_ref, v_ref, q_ref, kv_ref, ga_ref, gb_ref, h_ref):
    h_ref[...] = _rms(x_ref[...], g_ref[...]).astype(BF16)
    _project(h_ref, TM_MIX, w_ref, gm_ref, u_ref, v_ref, q_ref, kv_ref, ga_ref, gb_ref)


def _inproj(x, g, w_in, gm):
    n = x.shape[0]
    tm = TM_MIX
    row = lambda w: pl.BlockSpec((tm, w), lambda i: (i, 0))
    wide = jax.ShapeDtypeStruct((n, D_MODEL), BF16)
    return pl.pallas_call(
        _inproj_body,
        grid=(n // tm,),
        in_specs=[row(D_MODEL), _const_spec((1, D_MODEL)), _const_spec((D_MODEL, IN_COLS)),
                  _const_spec((1, GM_WIDTH))],
        out_specs=[row(GM_WIDTH), row(GM_WIDTH), row(QW), row(2 * KVW), row(D_MODEL), row(D_MODEL)],
        out_shape=[wide, wide, wide, jax.ShapeDtypeStruct((n, 2 * KVW), F32), wide, wide],
        scratch_shapes=[pltpu.VMEM((tm, D_MODEL), BF16)],
        compiler_params=_params(),
        name="inproj",
    )(x, g, w_in, gm)


def _stage_padded(src, dst_ref):
    rows = src.shape[0]
    low = lax.broadcasted_iota(jnp.int32, src.shape, 1) < HEAD_DIM
    rolled = pltpu.roll(src, HEAD_DIM, 1)
    dst_ref[0, :rows] = jnp.where(low, src, 0.0).astype(BF16)
    dst_ref[1, :rows] = jnp.where(low, 0.0, rolled).astype(BF16)
    dst_ref[2, :rows] = jnp.where(low, rolled, 0.0).astype(BF16)
    dst_ref[3, :rows] = jnp.where(low, 0.0, src).astype(BF16)


def _stage_kv(kv, kst_ref, vst_ref):
    rows = kv.shape[0]
    _stage_padded(kv[:, :KVW], kst_ref)
    _stage_padded(kv[:, KVW:], vst_ref)
    pad = kst_ref.shape[1] - rows
    kst_ref[:, rows:, :] = jnp.zeros((N_GROUPS, pad, LANES), BF16)


def _stage_sink_fill(sinks_ref, fill_ref, rows_per_slab, n_keys):
    shape = fill_ref.shape[1:]
    row = lax.broadcasted_iota(jnp.int32, shape, 0)
    lane = lax.broadcasted_iota(jnp.int32, shape, 1)
    for kh in range(N_KV_HEADS):
        for j in range(SLAB_HEADS):
            head = lambda p: kh * Q_PER_KV + SLAB_HEADS * p + j
            snk = jnp.full(shape, sinks_ref[head(KV_SLABS - 1)], F32)
            for p in range(KV_SLABS - 2, -1, -1):
                snk = jnp.where(row < (p + 1) * rows_per_slab, sinks_ref[head(p)], snk)
            fill_ref[kh * SLAB_HEADS + j] = jnp.where(lane == n_keys - LANES, snk, NEG)


def _attend(groups, n_keys):
    scores = [_dot_nt(g[0], g[1]) for g in groups]
    probs, rdens = [], []
    for s, (_, _, _, fill, mask_lo, mask_hi) in zip(scores, groups):
        s_lo, s_hi = s[:, :LANES], s[:, LANES:]
        if mask_lo is not None:
            s_lo = jnp.where(mask_lo, s_lo, NEG)
        s_hi = jnp.where(mask_hi, s_hi, fill)
        m = jnp.max(jnp.maximum(s_lo, s_hi), axis=1, keepdims=True)
        e_lo = jnp.exp(s_lo - m)
        e_hi = jnp.exp(s_hi - m)
        rdens.append(1.0 / jnp.sum(e_lo + e_hi, axis=1, keepdims=True))
        probs.append(jnp.concatenate([e_lo, e_hi], axis=1).astype(BF16))
    return [_dot(p[:, :n_keys], g[2]) * r for p, r, g in zip(probs, rdens, groups)]


def _slab(kh, p):
    return slice((kh * KV_SLABS + p) * LANES, (kh * KV_SLABS + p + 1) * LANES)


def _attend_rows(q_ref, ob_ref, rows, t, kst_ref, vst_ref, fill_ref, key0, n_keys, mask_lo, mask_hi):
    groups = []
    for kh in range(N_KV_HEADS):
        lhs = jnp.concatenate([q_ref[rows, _slab(kh, p)] for p in range(KV_SLABS)], axis=0)
        for j in range(SLAB_HEADS):
            g = kh * SLAB_HEADS + j
            groups.append((lhs, kst_ref[g, key0:key0 + KEY_TILE, :], vst_ref[g, key0:key0 + n_keys, :],
                           fill_ref[g], mask_lo, mask_hi))
    outs = _attend(groups, n_keys)
    for kh in range(N_KV_HEADS):
        o = outs[kh * SLAB_HEADS]
        for j in range(1, SLAB_HEADS):
            o = o + outs[kh * SLAB_HEADS + j]
        for p in range(KV_SLABS):
            ob_ref[rows, _slab(kh, p)] = o[p * t:(p + 1) * t].astype(BF16)


def _merge(x, oa, ob, ga, gb, wpa_ref, wpb_ref, wout_ref):
    ya = _dot(oa, wpa_ref[...])
    yb = _dot(ob, wpb_ref[...])
    m = (ga.astype(F32) * ya + gb.astype(F32) * yb).astype(BF16)
    return x + _dot(m, wout_ref[...])


def _mix_prompt_body(sinks_ref, x_ref, u_ref, v_ref, q_ref, kvp_ref, kvc_ref, ga_ref, gb_ref,
                     ws_ref, bst_ref, wpa_ref, wpb_ref, wout_ref, o_ref,
                     kst_ref, vt_ref, oa_ref, ob_ref, *, blocks_per_stream):
    tm = TM_MIX
    ti = lax.rem(pl.program_id(0), blocks_per_stream)

    n_gm = tm // GM_CHUNK
    ri = lax.broadcasted_iota(jnp.int32, (GM_CHUNK, GM_CHUNK), 0)
    ci = lax.broadcasted_iota(jnp.int32, (GM_CHUNK, GM_CHUNK), 1)
    gm_mask = _chunk_causal(ri, ci).astype(F32)
    for g in range(GM_GROUPS):
        cols = slice(g * GM_GROUP_DIM, (g + 1) * GM_GROUP_DIM)
        w = (ws_ref[g] * gm_mask).astype(BF16)
        rhs = jnp.concatenate(
            [v_ref[c * GM_CHUNK:(c + 1) * GM_CHUNK, cols] for c in range(n_gm)], axis=1)
        sg = _dot(w, rhs) + bst_ref[:, g:g + 1]
        for c in range(n_gm):
            rows = slice(c * GM_CHUNK, (c + 1) * GM_CHUNK)
            oa_ref[rows, cols] = (u_ref[rows, cols].astype(F32)
                                  * sg[:, c * GM_CHUNK:(c + 1) * GM_CHUNK]).astype(BF16)

    kv = jnp.concatenate([kvp_ref[...], kvc_ref[...]], axis=0)
    _stage_padded(kv[:, :KVW], kst_ref)
    v2 = kv[:, KVW:]
    vt_ref[0] = v2.T.astype(BF16)
    vt_ref[1] = jnp.concatenate([v2[CHUNK:], jnp.zeros((CHUNK, KVW), F32)], axis=0).T.astype(BF16)
    n_q = KV_SLABS * CHUNK
    qlane = lax.broadcasted_iota(jnp.int32, (1, n_q), 1)
    sink_rows = []
    for kh in range(N_KV_HEADS):
        for j in range(SLAB_HEADS):
            head = lambda p: kh * Q_PER_KV + SLAB_HEADS * p + j
            snk = jnp.full((1, n_q), sinks_ref[head(KV_SLABS - 1)], F32)
            for p in range(KV_SLABS - 2, -1, -1):
                snk = jnp.where(qlane < (p + 1) * CHUNK, sinks_ref[head(p)], snk)
            sink_rows.append(snk)
    units = [(c, kh) for c in range(tm // CHUNK) for kh in range(N_KV_HEADS)]
    for u0 in range(0, len(units), ATTN_BATCH):
        batch = units[u0:u0 + ATTN_BATCH]
        scores = []
        for c, kh in batch:
            r0 = c * CHUNK
            qs = jnp.concatenate([q_ref[r0:r0 + CHUNK, _slab(kh, p)] for p in range(KV_SLABS)], axis=0)
            keys = jnp.concatenate([kst_ref[kh * SLAB_HEADS + j, r0:r0 + BAND, :]
                                    for j in range(SLAB_HEADS)], axis=0)
            scores.append(_dot_nt(keys, qs))
        probs = []
        for (c, kh), st in zip(batch, scores):
            r0 = c * CHUNK
            for j in range(SLAB_HEADS):
                s = st[j * BAND:(j + 1) * BAND]
                if r0 < WINDOW:
                    key_ok = ti * tm + r0 - WINDOW + lax.broadcasted_iota(jnp.int32, (BAND, n_q), 0) >= 0
                    s = jnp.where(key_ok, s, NEG)
                snk = sink_rows[kh * SLAB_HEADS + j]
                m = jnp.maximum(jnp.max(s, axis=0, keepdims=True), snk)
                e = jnp.exp(s - m)
                den = jnp.sum(e, axis=0, keepdims=True) + jnp.exp(snk - m)
                probs.append((e.astype(BF16), 1.0 / den))
        for n, (c, kh) in enumerate(batch):
            r0 = c * CHUNK
            shifted = (r0 % LANES) // CHUNK
            v0 = r0 - shifted * CHUNK
            vt = vt_ref[shifted, kh * HEAD_DIM:(kh + 1) * HEAD_DIM, v0:v0 + BAND]
            outs = [_dot(vt, p) * r for p, r in probs[n * SLAB_HEADS:(n + 1) * SLAB_HEADS]]
            o = jnp.concatenate(outs, axis=0).T
            for p in range(KV_SLABS):
                ob_ref[r0:r0 + CHUNK, _slab(kh, p)] = o[p * CHUNK:(p + 1) * CHUNK].astype(BF16)

    o_ref[...] = _merge(x_ref[...], oa_ref[...], ob_ref[...], ga_ref[...], gb_ref[...],
                        wpa_ref, wpb_ref, wout_ref)


def _mix_prompt(sinks, x, u, v, q, kv, ga, gb, ws, bst, wpa, wpb, wout, blocks_per_stream):
    n = x.shape[0]
    tm = TM_MIX
    blk = lambda w: pl.BlockSpec((tm, w), lambda i, s: (i, 0))
    prev = pl.BlockSpec((WINDOW, 2 * KVW), lambda i, s: (jnp.maximum(i * (tm // WINDOW) - 1, 0), 0))
    grid_spec = pltpu.PrefetchScalarGridSpec(
        num_scalar_prefetch=1,
        grid=(n // tm,),
        in_specs=[blk(D_MODEL), blk(GM_WIDTH), blk(GM_WIDTH), blk(QW), prev, blk(2 * KVW),
                  blk(D_MODEL), blk(D_MODEL),
                  _const_spec((GM_GROUPS, GM_CHUNK, GM_CHUNK)), _const_spec((GM_CHUNK, GM_GROUPS)),
                  _const_spec((GM_WIDTH, D_MODEL)), _const_spec((QW, D_MODEL)),
                  _const_spec((D_MODEL, D_MODEL))],
        out_specs=blk(D_MODEL),
        scratch_shapes=[pltpu.VMEM((N_GROUPS, tm + WINDOW, LANES), BF16),
                        pltpu.VMEM((2, KVW, tm + WINDOW), BF16),
                        pltpu.VMEM((tm, GM_WIDTH), BF16),
                        pltpu.VMEM((tm, QW), BF16)],
    )
    return pl.pallas_call(
        functools.partial(_mix_prompt_body, blocks_per_stream=blocks_per_stream),
        grid_spec=grid_spec,
        out_shape=jax.ShapeDtypeStruct((n, D_MODEL), F32),
        compiler_params=_params(),
        name="mix_prompt",
    )(sinks, x, u, v, q, kv, kv, ga, gb, ws, bst, wpa, wpb, wout)


def _mix_sample_body(sinks_ref, xs_ref, ck_ref, cv_ref, nmix_ref, win_ref, gmn_ref, wst_ref, bsts_ref,
                     wpa_ref, wpb_ref, wout_ref, os_ref, kvs_ref, vns_ref,
                     h_ref, u_ref, q_ref, ga_ref, gb_ref,
                     kst_ref, vst_ref, fill_ref, oa_ref, ob_ref, *, nb, s, wc):
    n = nb * s
    x = xs_ref[...]
    h_ref[...] = _rms(x, nmix_ref[...]).astype(BF16)
    _project(h_ref, n, win_ref, gmn_ref, u_ref, vns_ref, q_ref, kvs_ref, ga_ref, gb_ref)

    ri = lax.broadcasted_iota(jnp.int32, (n, n), 0)
    ci = lax.broadcasted_iota(jnp.int32, (n, n), 1)
    gm_mask = ((_div(ri, s) == _div(ci, s)) & _chunk_causal(_mod(ri, s), _mod(ci, s))).astype(F32)
    for g in range(GM_GROUPS):
        cols = slice(g * GM_GROUP_DIM, (g + 1) * GM_GROUP_DIM)
        w = (wst_ref[g] * gm_mask).astype(BF16)
        sg = _dot(w, vns_ref[:, cols].astype(BF16)) + bsts_ref[:, g:g + 1]
        oa_ref[:, cols] = (u_ref[:, cols].astype(F32) * sg).astype(BF16)

    n_keys = wc + s
    _stage_sink_fill(sinks_ref, fill_ref, s, n_keys)
    rows = KV_SLABS * s
    q_pos = PAST_LEN + _mod(lax.broadcasted_iota(jnp.int32, (rows, KEY_TILE), 0), s)
    col = lax.broadcasted_iota(jnp.int32, (rows, KEY_TILE), 1)
    dist = _div(q_pos, CHUNK) - _div(PAST_LEN - wc + col, CHUNK)
    valid = (dist >= 0) & (dist <= WINDOW_CHUNKS) & (col < n_keys)
    for b in range(nb):
        r = slice(b * s, (b + 1) * s)
        new = kvs_ref[r, :]
        kv = jnp.concatenate(
            [jnp.concatenate([ck_ref[b], new[:, :KVW]], axis=0),
             jnp.concatenate([cv_ref[b], new[:, KVW:]], axis=0)], axis=1)
        _stage_kv(kv, kst_ref, vst_ref)
        _attend_rows(q_ref, ob_ref, r, s, kst_ref, vst_ref, fill_ref, 0, n_keys,
                     valid[:, :LANES], valid[:, LANES:])

    os_ref[...] = _merge(x, oa_ref[...], ob_ref[...], ga_ref[...], gb_ref[...],
                         wpa_ref, wpb_ref, wout_ref)


def _mix_sample(sinks, xs, ck, cv, nmix, win, gmn, wst, bsts, wpa, wpb, wout, nb, s):
    ns = xs.shape[0]
    wc = ck.shape[1]
    full = lambda shape: pl.BlockSpec(shape, lambda i, sk: (0,) * len(shape))
    args = (xs, ck, cv, nmix, win, gmn, wst, bsts, wpa, wpb, wout)
    wide = pltpu.VMEM((ns, D_MODEL), BF16)
    grid_spec = pltpu.PrefetchScalarGridSpec(
        num_scalar_prefetch=1,
        grid=(1,),
        in_specs=[full(a.shape) for a in args],
        out_specs=[full((ns, D_MODEL)), full((ns, 2 * KVW)), full((ns, GM_WIDTH))],
        scratch_shapes=[wide, wide, wide, wide, wide,
                        pltpu.VMEM((N_GROUPS, KEY_TILE, LANES), BF16),
                        pltpu.VMEM((N_GROUPS, wc + s, LANES), BF16),
                        pltpu.VMEM((N_GROUPS, KV_SLABS * s, LANES), F32),
                        wide, wide],
    )
    return pl.pallas_call(
        functools.partial(_mix_sample_body, nb=nb, s=s, wc=wc),
        grid_spec=grid_spec,
        out_shape=[jax.ShapeDtypeStruct((ns, D_MODEL), F32),
                   jax.ShapeDtypeStruct((ns, 2 * KVW), F32),
                   jax.ShapeDtypeStruct((ns, GM_WIDTH), F32)],
        compiler_params=_params(),
        name="mix_sample",
    )(sinks, *args)


def kernel(x_prompt, x_sample, cache_k, cache_v, norm_ffn1, ffn1_w1, ffn1_w3, ffn1_w2, norm_mix, w_in, gm_norm, gm_ws, gm_bs, sinks, w_pa, w_pb, w_out, norm_ffn2, ffn2_w1, ffn2_w3, ffn2_w2, norm_final):
    depth = norm_ffn1.shape[0]
    b, t, _ = x_prompt.shape
    nb, s, _ = x_sample.shape
    wc = cache_k.shape[2]
    assert t % TM_MIX == 0 and (b * t) % TM_FFN == 0 and t >= WINDOW and TM_MIX % WINDOW == 0
    assert wc + s < KEY_TILE and wc % 16 == 0 and s % 16 == 0
    row = lambda a: a.reshape(1, -1).astype(F32)
    gf = row(norm_final)

    xp = x_prompt.reshape(b * t, D_MODEL)
    xs = x_sample.reshape(nb * s, D_MODEL)
    kp_l, vp_l, ks_l, vs_l, gs_l = [], [], [], [], []
    for l in range(depth):
        last = l == depth - 1
        ws = gm_ws[l].astype(F32)
        bs = gm_bs[l].astype(F32)
        snk = sinks[l].astype(F32)
        nmix, gmn, win = row(norm_mix[l]), row(gm_norm[l]), w_in[l].astype(BF16)
        wpa, wpb, wout = w_pa[l].astype(BF16), w_pb[l].astype(BF16), w_out[l].astype(BF16)

        xp, xs = _ffn(xp, xs, row(norm_ffn1[l]), ffn1_w1[l].astype(BF16), ffn1_w3[l].astype(BF16),
                      ffn1_w2[l].astype(BF16), None)
        u, v, q, kv, ga, gb = _inproj(xp, nmix, win, gmn)
        xp = _mix_prompt(snk, xp, u, v, q, kv, ga, gb, ws, bs.T, wpa, wpb, wout, t // TM_MIX)
        xs, kv_s, vn_s = _mix_sample(
            snk, xs, cache_k[l].reshape(nb, wc, KVW), cache_v[l].reshape(nb, wc, KVW), nmix, win, gmn,
            jnp.tile(ws[:, :s, :s], (1, nb, nb)), jnp.tile(bs[:, :s].T, (nb, 1)), wpa, wpb, wout, nb, s)
        xp, xs = _ffn(xp, xs, row(norm_ffn2[l]), ffn2_w1[l].astype(BF16), ffn2_w3[l].astype(BF16),
                      ffn2_w2[l].astype(BF16), gf if last else None)

        keep = min(WINDOW, t)
        kv_tail = kv.reshape(b, t, 2 * KVW)[:, t - keep:]
        kp_l.append(kv_tail[:, :, :KVW].reshape(b, keep, N_KV_HEADS, HEAD_DIM))
        vp_l.append(kv_tail[:, :, KVW:].reshape(b, keep, N_KV_HEADS, HEAD_DIM))
        ks_l.append(kv_s[:, :KVW].reshape(nb, s, N_KV_HEADS, HEAD_DIM))
        vs_l.append(kv_s[:, KVW:].reshape(nb, s, N_KV_HEADS, HEAD_DIM))
        gs_l.append(vn_s.reshape(nb, s, GM_WIDTH))

    return (xp.reshape(b, t, D_MODEL), xs.reshape(nb, s, D_MODEL), jnp.stack(kp_l), jnp.stack(vp_l),
            jnp.stack(ks_l), jnp.stack(vs_l), jnp.stack(gs_l))
```

```python
import functools

import jax
import jax.numpy as jnp
from jax import lax
from jax.experimental import pallas as pl
from jax.experimental.pallas import tpu as pltpu

D_MODEL = 1024
PAST_LEN = 1024
CHUNK = 64
GM_CHUNK = 128
GM_GROUPS = 8
GM_WIDTH = 1024
GM_GROUP_DIM = GM_WIDTH // GM_GROUPS
N_HEADS = 16
N_KV_HEADS = 2
HEAD_DIM = 64
Q_PER_KV = N_HEADS // N_KV_HEADS
WINDOW = 128
WINDOW_CHUNKS = WINDOW // CHUNK
BAND = WINDOW + CHUNK
QW = N_HEADS * HEAD_DIM
KVW = N_KV_HEADS * HEAD_DIM
D_FF = 2816
EPS = 1e-6
NEG = -1e30
SCALE = HEAD_DIM ** -0.5
LOG2E = 1.4426950408889634

C_U = 0
C_V = GM_WIDTH
C_Q = 2 * GM_WIDTH
C_KV = C_Q + QW
C_GA = C_KV + 2 * KVW
C_GB = C_GA + D_MODEL
IN_COLS = C_GB + D_MODEL

LANES = 128
SLAB_HEADS = LANES // HEAD_DIM
KV_SLABS = Q_PER_KV // SLAB_HEADS
N_GROUPS = N_KV_HEADS * SLAB_HEADS
KEY_TILE = 2 * LANES
FF_TILE = 256
VMEM_LIMIT = 56 * 1024 * 1024
TM_FFN = 1024
TM_MIX = 512
VT_ROWS = HEAD_DIM + 16
ATTN_BATCH = 16

F32 = jnp.float32
BF16 = jnp.bfloat16


def _rms(x, g):
    return x * lax.rsqrt(jnp.mean(x * x, axis=-1, keepdims=True) + EPS) * g


def _dot(a, b):
    return jnp.dot(a, b, preferred_element_type=F32)


def _dot_nt(a, b):
    return lax.dot_general(a, b, (((1,), (1,)), ((), ())), preferred_element_type=F32)


def _const_spec(shape):
    zeros = (0,) * len(shape)
    return pl.BlockSpec(shape, lambda *_: zeros, pipeline_mode=pl.Buffered(1))


def _params():
    return pltpu.CompilerParams(dimension_semantics=("arbitrary",), vmem_limit_bytes=VMEM_LIMIT)


def _div(x, d):
    assert d & (d - 1) == 0
    return lax.shift_right_logical(x, d.bit_length() - 1)


def _mod(x, d):
    assert d & (d - 1) == 0
    return x & (d - 1)


def _chunk_causal(i, j):
    return _div(i, CHUNK) >= _div(j, CHUNK)


def _ffn_rows(x, g_ref, w1_ref, w3_ref, w2_ref, gf_ref):
    h = _rms(x, g_ref[...]).astype(BF16)
    acc = jnp.zeros(x.shape, F32)
    for f in range(D_FF // FF_TILE):
        sl = slice(f * FF_TILE, (f + 1) * FF_TILE)
        a = _dot(h, w1_ref[:, sl])
        b = _dot(h, w3_ref[:, sl])
        gated = (a * jax.nn.sigmoid(a) * b).astype(BF16)
        acc = acc + _dot(gated, w2_ref[sl, :])
    y = x + 0.5 * acc
    return y if gf_ref is None else _rms(y, gf_ref[...])


def _ffn_body(x_ref, xs_ref, g_ref, w1_ref, w3_ref, w2_ref, *rest, final, n_blocks):
    gf_ref = rest[0] if final else None
    o_ref, os_ref = rest[-2:]
    i = pl.program_id(0)

    @pl.when(i < n_blocks)
    def _():
        o_ref[...] = _ffn_rows(x_ref[...], g_ref, w1_ref, w3_ref, w2_ref, gf_ref)

    @pl.when(i == n_blocks)
    def _():
        os_ref[...] = _ffn_rows(xs_ref[...], g_ref, w1_ref, w3_ref, w2_ref, gf_ref)


def _ffn(x, xs, g, w1, w3, w2, gf):
    n, ns = x.shape[0], xs.shape[0]
    n_blocks = n // TM_FFN
    final = gf is not None
    row = pl.BlockSpec((TM_FFN, D_MODEL), lambda i: (jnp.minimum(i, n_blocks - 1), 0))
    in_specs = [row, _const_spec((ns, D_MODEL)), _const_spec((1, D_MODEL)),
                _const_spec((D_MODEL, D_FF)), _const_spec((D_MODEL, D_FF)),
                _const_spec((D_FF, D_MODEL))]
    args = [x, xs, g, w1, w3, w2]
    if final:
        in_specs.append(_const_spec((1, D_MODEL)))
        args.append(gf)
    return pl.pallas_call(
        functools.partial(_ffn_body, final=final, n_blocks=n_blocks),
        grid=(n_blocks + 1,),
        in_specs=in_specs,
        out_specs=[row, pl.BlockSpec((ns, D_MODEL), lambda i: (0, 0))],
        out_shape=[jax.ShapeDtypeStruct((n, D_MODEL), F32), jax.ShapeDtypeStruct((ns, D_MODEL), F32)],
        compiler_params=_params(),
        name="ffn_final" if final else "ffn",
    )(*args)


def _project(h_ref, n, win_ref, gmn_ref, u_ref, v_ref, q_ref, kv_ref, ga_ref, gb_ref):
    def proj(lo, hi):
        return _dot(h_ref[:n], win_ref[:, lo:hi])

    u_ref[:n] = jax.nn.gelu(proj(C_U, C_V)).astype(u_ref.dtype)
    v_ref[:n] = _rms(jax.nn.gelu(proj(C_V, C_Q)), gmn_ref[...]).astype(v_ref.dtype)
    ga_ref[:n] = jax.nn.sigmoid(proj(C_GA, C_GB)).astype(ga_ref.dtype)
    gb_ref[:n] = jax.nn.sigmoid(proj(C_GB, IN_COLS)).astype(gb_ref.dtype)
    q_ref[:n] = (proj(C_Q, C_KV) * (SCALE * LOG2E)).astype(q_ref.dtype)
    kv_ref[:n] = proj(C_KV, C_GA)


def _inproj_body(x_ref, g_ref, w_ref, gm_ref, u_ref, v_ref, q_ref, kv_ref, ga_ref, gb_ref, h_ref):
    h_ref[...] = _rms(x_ref[...], g_ref[...]).astype(BF16)
    _project(h_ref, TM_MIX, w_ref, gm_ref, u_ref, v_ref, q_ref, kv_ref, ga_ref, gb_ref)


def _inproj(x, g, w_in, gm):
    n = x.shape[0]
    tm = TM_MIX
    row = lambda w: pl.BlockSpec((tm, w), lambda i: (i, 0))
    wide = jax.ShapeDtypeStruct((n, D_MODEL), BF16)
    return pl.pallas_call(
        _inproj_body,
        grid=(n // tm,),
        in_specs=[row(D_MODEL), _const_spec((1, D_MODEL)), _const_spec((D_MODEL, IN_COLS)),
                  _const_spec((1, GM_WIDTH))],
        out_specs=[row(GM_WIDTH), row(GM_WIDTH), row(QW), row(2 * KVW), row(D_MODEL), row(D_MODEL)],
        out_shape=[wide, wide, wide, jax.ShapeDtypeStruct((n, 2 * KVW), F32), wide, wide],
        scratch_shapes=[pltpu.VMEM((tm, D_MODEL), BF16)],
        compiler_params=_params(),
        name="inproj",
    )(x, g, w_in, gm)


def _stage_padded(src, dst_ref):
    rows = src.shape[0]
    low = lax.broadcasted_iota(jnp.int32, src.shape, 1) < HEAD_DIM
    rolled = pltpu.roll(src, HEAD_DIM, 1)
    dst_ref[0, :rows] = jnp.where(low, src, 0.0).astype(BF16)
    dst_ref[1, :rows] = jnp.where(low, 0.0, rolled).astype(BF16)
    dst_ref[2, :rows] = jnp.where(low, rolled, 0.0).astype(BF16)
    dst_ref[3, :rows] = jnp.where(low, 0.0, src).astype(BF16)


def _stage_kv(kv, kst_ref, vst_ref):
    rows = kv.shape[0]
    _stage_padded(kv[:, :KVW], kst_ref)
    _stage_padded(kv[:, KVW:], vst_ref)
    pad = kst_ref.shape[1] - rows
    kst_ref[:, rows:, :] = jnp.zeros((N_GROUPS, pad, LANES), BF16)


def _stage_sink_fill(sinks_ref, fill_ref, rows_per_slab, n_keys):
    shape = fill_ref.shape[1:]
    row = lax.broadcasted_iota(jnp.int32, shape, 0)
    lane = lax.broadcasted_iota(jnp.int32, shape, 1)
    for kh in range(N_KV_HEADS):
        for j in range(SLAB_HEADS):
            head = lambda p: kh * Q_PER_KV + SLAB_HEADS * p + j
            snk = jnp.full(shape, sinks_ref[head(KV_SLABS - 1)] * LOG2E, F32)
            for p in range(KV_SLABS - 2, -1, -1):
                snk = jnp.where(row < (p + 1) * rows_per_slab, sinks_ref[head(p)] * LOG2E, snk)
            fill_ref[kh * SLAB_HEADS + j] = jnp.where(lane == n_keys - LANES, snk, NEG)


def _attend(groups, n_keys):
    scores = [_dot_nt(g[0], g[1]) for g in groups]
    probs, rdens = [], []
    for s, (_, _, _, fill, mask_lo, mask_hi) in zip(scores, groups):
        s_lo, s_hi = s[:, :LANES], s[:, LANES:]
        if mask_lo is not None:
            s_lo = jnp.where(mask_lo, s_lo, NEG)
        s_hi = jnp.where(mask_hi, s_hi, fill)
        m = jnp.max(jnp.maximum(s_lo, s_hi), axis=1, keepdims=True)
        e_lo = jnp.exp2(s_lo - m)
        e_hi = jnp.exp2(s_hi - m)
        rdens.append(1.0 / jnp.sum(e_lo + e_hi, axis=1, keepdims=True))
        probs.append(jnp.concatenate([e_lo, e_hi], axis=1).astype(BF16))
    return [_dot(p[:, :n_keys], g[2]) * r for p, r, g in zip(probs, rdens, groups)]


def _slab(kh, p):
    return slice((kh * KV_SLABS + p) * LANES, (kh * KV_SLABS + p + 1) * LANES)


def _attend_rows(q_ref, ob_ref, rows, t, kst_ref, vst_ref, fill_ref, key0, n_keys, mask_lo, mask_hi):
    groups = []
    for kh in range(N_KV_HEADS):
        lhs = jnp.concatenate([q_ref[rows, _slab(kh, p)] for p in range(KV_SLABS)], axis=0)
        for j in range(SLAB_HEADS):
            g = kh * SLAB_HEADS + j
            groups.append((lhs, kst_ref[g, key0:key0 + KEY_TILE, :], vst_ref[g, key0:key0 + n_keys, :],
                           fill_ref[g], mask_lo, mask_hi))
    outs = _attend(groups, n_keys)
    for kh in range(N_KV_HEADS):
        o = outs[kh * SLAB_HEADS]
        for j in range(1, SLAB_HEADS):
            o = o + outs[kh * SLAB_HEADS + j]
        for p in range(KV_SLABS):
            ob_ref[rows, _slab(kh, p)] = o[p * t:(p + 1) * t].astype(BF16)


def _merge(x, oa, ob, ga, gb, wpa_ref, wpb_ref, wout_ref):
    ya = _dot(oa, wpa_ref[...])
    yb = _dot(ob, wpb_ref[...])
    m = (ga.astype(F32) * ya + gb.astype(F32) * yb).astype(BF16)
    return x + _dot(m, wout_ref[...])


def _mix_prompt_body(sinks_ref, x_ref, u_ref, v_ref, q_ref, kvp_ref, kvc_ref, ga_ref, gb_ref,
                     ws_ref, bst_ref, wpa_ref, wpb_ref, wout_ref, o_ref,
                     kst_ref, vt_ref, oa_ref, ob_ref, *, blocks_per_stream):
    tm = TM_MIX
    ti = lax.rem(pl.program_id(0), blocks_per_stream)

    n_gm = tm // GM_CHUNK
    ri = lax.broadcasted_iota(jnp.int32, (GM_CHUNK, GM_CHUNK), 0)
    ci = lax.broadcasted_iota(jnp.int32, (GM_CHUNK, GM_CHUNK), 1)
    gm_mask = _chunk_causal(ri, ci).astype(F32)
    for g in range(GM_GROUPS):
        cols = slice(g * GM_GROUP_DIM, (g + 1) * GM_GROUP_DIM)
        w = (ws_ref[g] * gm_mask).astype(BF16)
        rhs = jnp.concatenate(
            [v_ref[c * GM_CHUNK:(c + 1) * GM_CHUNK, cols] for c in range(n_gm)], axis=1)
        sg = _dot(w, rhs) + bst_ref[:, g:g + 1]
        for c in range(n_gm):
            rows = slice(c * GM_CHUNK, (c + 1) * GM_CHUNK)
            oa_ref[rows, cols] = (u_ref[rows, cols].astype(F32)
                                  * sg[:, c * GM_CHUNK:(c + 1) * GM_CHUNK]).astype(BF16)

    kv = jnp.concatenate([kvp_ref[...], kvc_ref[...]], axis=0)
    _stage_padded(kv[:, :KVW], kst_ref)
    v2 = kv[:, KVW:]
    ones_row = (lax.broadcasted_iota(jnp.int32, (VT_ROWS - HEAD_DIM, tm + WINDOW), 0) == 0).astype(BF16)
    v2s = jnp.concatenate([v2[CHUNK:], jnp.zeros((CHUNK, KVW), F32)], axis=0)
    for shifted, vt in enumerate((v2.T.astype(BF16), v2s.T.astype(BF16))):
        for kh in range(N_KV_HEADS):
            vt_ref[shifted, kh, :HEAD_DIM] = vt[kh * HEAD_DIM:(kh + 1) * HEAD_DIM]
            vt_ref[shifted, kh, HEAD_DIM:] = ones_row
    n_q = KV_SLABS * CHUNK
    qlane = lax.broadcasted_iota(jnp.int32, (1, n_q), 1)
    sink_rows = []
    for kh in range(N_KV_HEADS):
        for j in range(SLAB_HEADS):
            head = lambda p: kh * Q_PER_KV + SLAB_HEADS * p + j
            snk = jnp.full((1, n_q), sinks_ref[head(KV_SLABS - 1)] * LOG2E, F32)
            for p in range(KV_SLABS - 2, -1, -1):
                snk = jnp.where(qlane < (p + 1) * CHUNK, sinks_ref[head(p)] * LOG2E, snk)
            sink_rows.append(snk)
    units = [(c, kh) for c in range(tm // CHUNK) for kh in range(N_KV_HEADS)]
    for u0 in range(0, len(units), ATTN_BATCH):
        batch = units[u0:u0 + ATTN_BATCH]
        scores = []
        for c, kh in batch:
            r0 = c * CHUNK
            qs = jnp.concatenate([q_ref[r0:r0 + CHUNK, _slab(kh, p)] for p in range(KV_SLABS)], axis=0)
            keys = jnp.concatenate([kst_ref[kh * SLAB_HEADS + j, r0:r0 + BAND, :]
                                    for j in range(SLAB_HEADS)], axis=0)
            scores.append(_dot_nt(keys, qs))
        probs = []
        for (c, kh), st in zip(batch, scores):
            r0 = c * CHUNK
            for j in range(SLAB_HEADS):
                s = st[j * BAND:(j + 1) * BAND]
                if r0 < WINDOW:
                    key_ok = ti * tm + r0 - WINDOW + lax.broadcasted_iota(jnp.int32, (BAND, n_q), 0) >= 0
                    s = jnp.where(key_ok, s, NEG)
                snk = sink_rows[kh * SLAB_HEADS + j]
                m = jnp.maximum(jnp.max(s, axis=0, keepdims=True), snk)
                probs.append((jnp.exp2(s - m).astype(BF16), jnp.exp2(snk - m)))
        for n, (c, kh) in enumerate(batch):
            r0 = c * CHUNK
            shifted = (r0 % LANES) // CHUNK
            v0 = r0 - shifted * CHUNK
            vt = vt_ref[shifted, kh, :, v0:v0 + BAND]
            outs = []
            for p, e_sink in probs[n * SLAB_HEADS:(n + 1) * SLAB_HEADS]:
                pv = _dot(vt, p)
                den = pv[HEAD_DIM:HEAD_DIM + 1] + e_sink
                outs.append(pv[:HEAD_DIM] * (1.0 / den))
            o = jnp.concatenate(outs, axis=0).T
            for p in range(KV_SLABS):
                ob_ref[r0:r0 + CHUNK, _slab(kh, p)] = o[p * CHUNK:(p + 1) * CHUNK].astype(BF16)

    o_ref[...] = _merge(x_ref[...], oa_ref[...], ob_ref[...], ga_ref[...], gb_ref[...],
                        wpa_ref, wpb_ref, wout_ref)


def _mix_prompt(sinks, x, u, v, q, kv, ga, gb, ws, bst, wpa, wpb, wout, blocks_per_stream):
    n = x.shape[0]
    tm = TM_MIX
    blk = lambda w: pl.BlockSpec((tm, w), lambda i, s: (i, 0))
    prev = pl.BlockSpec((WINDOW, 2 * KVW), lambda i, s: (jnp.maximum(i * (tm // WINDOW) - 1, 0), 0))
    grid_spec = pltpu.PrefetchScalarGridSpec(
        num_scalar_prefetch=1,
        grid=(n // tm,),
        in_specs=[blk(D_MODEL), blk(GM_WIDTH), blk(GM_WIDTH), blk(QW), prev, blk(2 * KVW),
                  blk(D_MODEL), blk(D_MODEL),
                  _const_spec((GM_GROUPS, GM_CHUNK, GM_CHUNK)), _const_spec((GM_CHUNK, GM_GROUPS)),
                  _const_spec((GM_WIDTH, D_MODEL)), _const_spec((QW, D_MODEL)),
                  _const_spec((D_MODEL, D_MODEL))],
        out_specs=blk(D_MODEL),
        scratch_shapes=[pltpu.VMEM((N_GROUPS, tm + WINDOW, LANES), BF16),
                        pltpu.VMEM((2, N_KV_HEADS, VT_ROWS, tm + WINDOW), BF16),
                        pltpu.VMEM((tm, GM_WIDTH), BF16),
                        pltpu.VMEM((tm, QW), BF16)],
    )
    return pl.pallas_call(
        functools.partial(_mix_prompt_body, blocks_per_stream=blocks_per_stream),
        grid_spec=grid_spec,
        out_shape=jax.ShapeDtypeStruct((n, D_MODEL), F32),
        compiler_params=_params(),
        name="mix_prompt",
    )(sinks, x, u, v, q, kv, kv, ga, gb, ws, bst, wpa, wpb, wout)


def _mix_sample_body(sinks_ref, xs_ref, ck_ref, cv_ref, nmix_ref, win_ref, gmn_ref, wst_ref, bsts_ref,
                     wpa_ref, wpb_ref, wout_ref, os_ref, kvs_ref, vns_ref,
                     h_ref, u_ref, q_ref, ga_ref, gb_ref,
                     kst_ref, vst_ref, fill_ref, oa_ref, ob_ref, *, nb, s, wc):
    n = nb * s
    x = xs_ref[...]
    h_ref[...] = _rms(x, nmix_ref[...]).astype(BF16)
    _project(h_ref, n, win_ref, gmn_ref, u_ref, vns_ref, q_ref, kvs_ref, ga_ref, gb_ref)

    ri = lax.broadcasted_iota(jnp.int32, (n, n), 0)
    ci = lax.broadcasted_iota(jnp.int32, (n, n), 1)
    gm_mask = ((_div(ri, s) == _div(ci, s)) & _chunk_causal(_mod(ri, s), _mod(ci, s))).astype(F32)
    for g in range(GM_GROUPS):
        cols = slice(g * GM_GROUP_DIM, (g + 1) * GM_GROUP_DIM)
        w = (wst_ref[g] * gm_mask).astype(BF16)
        sg = _dot(w, vns_ref[:, cols].astype(BF16)) + bsts_ref[:, g:g + 1]
        oa_ref[:, cols] = (u_ref[:, cols].astype(F32) * sg).astype(BF16)

    n_keys = wc + s
    _stage_sink_fill(sinks_ref, fill_ref, s, n_keys)
    rows = KV_SLABS * s
    q_pos = PAST_LEN + _mod(lax.broadcasted_iota(jnp.int32, (rows, KEY_TILE), 0), s)
    col = lax.broadcasted_iota(jnp.int32, (rows, KEY_TILE), 1)
    dist = _div(q_pos, CHUNK) - _div(PAST_LEN - wc + col, CHUNK)
    valid = (dist >= 0) & (dist <= WINDOW_CHUNKS) & (col < n_keys)
    for b in range(nb):
        r = slice(b * s, (b + 1) * s)
        new = kvs_ref[r, :]
        kv = jnp.concatenate(
            [jnp.concatenate([ck_ref[b], new[:, :KVW]], axis=0),
             jnp.concatenate([cv_ref[b], new[:, KVW:]], axis=0)], axis=1)
        _stage_kv(kv, kst_ref, vst_ref)
        _attend_rows(q_ref, ob_ref, r, s, kst_ref, vst_ref, fill_ref, 0, n_keys,
                     valid[:, :LANES], valid[:, LANES:])

    os_ref[...] = _merge(x, oa_ref[...], ob_ref[...], ga_ref[...], gb_ref[...],
                         wpa_ref, wpb_ref, wout_ref)


def _mix_sample(sinks, xs, ck, cv, nmix, win, gmn, wst, bsts, wpa, wpb, wout, nb, s):
    ns = xs.shape[0]
    wc = ck.shape[1]
    full = lambda shape: pl.BlockSpec(shape, lambda i, sk: (0,) * len(shape))
    args = (xs, ck, cv, nmix, win, gmn, wst, bsts, wpa, wpb, wout)
    wide = pltpu.VMEM((ns, D_MODEL), BF16)
    grid_spec = pltpu.PrefetchScalarGridSpec(
        num_scalar_prefetch=1,
        grid=(1,),
        in_specs=[full(a.shape) for a in args],
        out_specs=[full((ns, D_MODEL)), full((ns, 2 * KVW)), full((ns, GM_WIDTH))],
        scratch_shapes=[wide, wide, wide, wide, wide,
                        pltpu.VMEM((N_GROUPS, KEY_TILE, LANES), BF16),
                        pltpu.VMEM((N_GROUPS, wc + s, LANES), BF16),
                        pltpu.VMEM((N_GROUPS, KV_SLABS * s, LANES), F32),
                        wide, wide],
    )
    return pl.pallas_call(
        functools.partial(_mix_sample_body, nb=nb, s=s, wc=wc),
        grid_spec=grid_spec,
        out_shape=[jax.ShapeDtypeStruct((ns, D_MODEL), F32),
                   jax.ShapeDtypeStruct((ns, 2 * KVW), F32),
                   jax.ShapeDtypeStruct((ns, GM_WIDTH), F32)],
        compiler_params=_params(),
        name="mix_sample",
    )(sinks, *args)


def kernel(x_prompt, x_sample, cache_k, cache_v, norm_ffn1, ffn1_w1, ffn1_w3, ffn1_w2, norm_mix, w_in, gm_norm, gm_ws, gm_bs, sinks, w_pa, w_pb, w_out, norm_ffn2, ffn2_w1, ffn2_w3, ffn2_w2, norm_final):
    depth = norm_ffn1.shape[0]
    b, t, _ = x_prompt.shape
    nb, s, _ = x_sample.shape
    wc = cache_k.shape[2]
    assert t % TM_MIX == 0 and (b * t) % TM_FFN == 0 and t >= WINDOW and TM_MIX % WINDOW == 0
    assert wc + s < KEY_TILE and wc % 16 == 0 and s % 16 == 0
    row = lambda a: a.reshape(1, -1).astype(F32)
    gf = row(norm_final)

    xp = x_prompt.reshape(b * t, D_MODEL)
    xs = x_sample.reshape(nb * s, D_MODEL)
    kp_l, vp_l, ks_l, vs_l, gs_l = [], [], [], [], []
    for l in range(depth):
        last = l == depth - 1
        ws = gm_ws[l].astype(F32)
        bs = gm_bs[l].astype(F32)
        snk = sinks[l].astype(F32)
        nmix, gmn, win = row(norm_mix[l]), row(gm_norm[l]), w_in[l].astype(BF16)
        wpa, wpb, wout = w_pa[l].astype(BF16), w_pb[l].astype(BF16), w_out[l].astype(BF16)

        xp, xs = _ffn(xp, xs, row(norm_ffn1[l]), ffn1_w1[l].astype(BF16), ffn1_w3[l].astype(BF16),
                      ffn1_w2[l].astype(BF16), None)
        u, v, q, kv, ga, gb = _inproj(xp, nmix, win, gmn)
        xp = _mix_prompt(snk, xp, u, v, q, kv, ga, gb, ws, bs.T, wpa, wpb, wout, t // TM_MIX)
        xs, kv_s, vn_s = _mix_sample(
            snk, xs, cache_k[l].reshape(nb, wc, KVW), cache_v[l].reshape(nb, wc, KVW), nmix, win, gmn,
            jnp.tile(ws[:, :s, :s], (1, nb, nb)), jnp.tile(bs[:, :s].T, (nb, 1)), wpa, wpb, wout, nb, s)
        xp, xs = _ffn(xp, xs, row(norm_ffn2[l]), ffn2_w1[l].astype(BF16), ffn2_w3[l].astype(BF16),
                      ffn2_w2[l].astype(BF16), gf if last else None)

        keep = min(WINDOW, t)
        kv_tail = kv.reshape(b, t, 2 * KVW)[:, t - keep:]
        kp_l.append(kv_tail[:, :, :KVW].reshape(b, keep, N_KV_HEADS, HEAD_DIM))
        vp_l.append(kv_tail[:, :, KVW:].reshape(b, keep, N_KV_HEADS, HEAD_DIM))
        ks_l.append(kv_s[:, :KVW].reshape(nb, s, N_KV_HEADS, HEAD_DIM))
        vs_l.append(kv_s[:, KVW:].reshape(nb, s, N_KV_HEADS, HEAD_DIM))
        gs_l.append(vn_s.reshape(nb, s, GM_WIDTH))

    return (xp.reshape(b, t, D_MODEL), xs.reshape(nb, s, D_MODEL), jnp.stack(kp_l), jnp.stack(vp_l),
            jnp.stack(ks_l), jnp.stack(vs_l), jnp.stack(gs_l))
```

```python
import functools

import jax
import jax.numpy as jnp
from jax import lax
from jax.experimental import pallas as pl
from jax.experimental.pallas import tpu as pltpu

D_MODEL = 1024
PAST_LEN = 1024
CHUNK = 64
GM_CHUNK = 128
GM_GROUPS = 8
GM_WIDTH = 1024
GM_GROUP_DIM = GM_WIDTH // GM_GROUPS
N_HEADS = 16
N_KV_HEADS = 2
HEAD_DIM = 64
Q_PER_KV = N_HEADS // N_KV_HEADS
WINDOW = 128
WINDOW_CHUNKS = WINDOW // CHUNK
BAND = WINDOW + CHUNK
QW = N_HEADS * HEAD_DIM
KVW = N_KV_HEADS * HEAD_DIM
D_FF = 2816
EPS = 1e-6
NEG = -1e30
SCALE = HEAD_DIM ** -0.5
LOG2E = 1.4426950408889634

C_U = 0
C_V = GM_WIDTH
C_Q = 2 * GM_WIDTH
C_KV = C_Q + QW
C_GA = C_KV + 2 * KVW
C_GB = C_GA + D_MODEL
IN_COLS = C_GB + D_MODEL

LANES = 128
BF16_SUBLANES = 16
SLAB_HEADS = LANES // HEAD_DIM
KV_SLABS = Q_PER_KV // SLAB_HEADS
N_GROUPS = N_KV_HEADS * SLAB_HEADS
KEY_TILE = 2 * LANES
FF_TILE = 256
VMEM_LIMIT = 56 * 1024 * 1024
TM_FFN = 1024
TM_MIX = 512
VT_ROWS = HEAD_DIM + 16
ATTN_BATCH = 16

F32 = jnp.float32
BF16 = jnp.bfloat16


def _rms(x, g):
    return x * lax.rsqrt(jnp.mean(x * x, axis=-1, keepdims=True) + EPS) * g


def _dot(a, b):
    return jnp.dot(a, b, preferred_element_type=F32)


def _dot_nt(a, b):
    return lax.dot_general(a, b, (((1,), (1,)), ((), ())), preferred_element_type=F32)


def _const_spec(shape):
    zeros = (0,) * len(shape)
    return pl.BlockSpec(shape, lambda *_: zeros, pipeline_mode=pl.Buffered(1))


def _params():
    return pltpu.CompilerParams(dimension_semantics=("arbitrary",), vmem_limit_bytes=VMEM_LIMIT)


def _div(x, d):
    assert d & (d - 1) == 0
    return lax.shift_right_logical(x, d.bit_length() - 1)


def _mod(x, d):
    assert d & (d - 1) == 0
    return x & (d - 1)


def _chunk_causal(i, j):
    return _div(i, CHUNK) >= _div(j, CHUNK)


def _ffn_rows(x, g_ref, w1_ref, w3_ref, w2_ref, gf_ref):
    h = _rms(x, g_ref[...]).astype(BF16)
    acc = jnp.zeros(x.shape, F32)
    for f in range(D_FF // FF_TILE):
        sl = slice(f * FF_TILE, (f + 1) * FF_TILE)
        a = _dot(h, w1_ref[:, sl])
        b = _dot(h, w3_ref[:, sl])
        gated = (a * jax.nn.sigmoid(a) * b).astype(BF16)
        acc = acc + _dot(gated, w2_ref[sl, :])
    y = x + 0.5 * acc
    return y if gf_ref is None else _rms(y, gf_ref[...])


def _ffn_body(x_ref, xs_ref, g_ref, w1_ref, w3_ref, w2_ref, *rest, final, n_blocks, cast_blocks):
    n_cast = len(cast_blocks)
    gf_ref = rest[0] if final else None
    cast_in = rest[int(final):int(final) + n_cast]
    o_ref, os_ref = rest[int(final) + n_cast:int(final) + n_cast + 2]
    cast_out = rest[int(final) + n_cast + 2:]
    i = pl.program_id(0)

    @pl.when(i < n_blocks)
    def _():
        o_ref[...] = _ffn_rows(x_ref[...], g_ref, w1_ref, w3_ref, w2_ref, gf_ref)

    @pl.when(i == n_blocks)
    def _():
        os_ref[...] = _ffn_rows(xs_ref[...], g_ref, w1_ref, w3_ref, w2_ref, gf_ref)

    for src_ref, dst_ref, blocks in zip(cast_in, cast_out, cast_blocks):
        @pl.when(i < blocks)
        def _():
            dst_ref[...] = src_ref[...].astype(BF16)


def _cast_rows(rows, n_steps):
    rb = BF16_SUBLANES
    while rows % rb or rows // rb > n_steps:
        rb += BF16_SUBLANES
        assert rb <= rows
    return rb


def _ffn(x, xs, g, w1, w3, w2, gf, cast=()):
    n, ns = x.shape[0], xs.shape[0]
    n_blocks = n // TM_FFN
    final = gf is not None
    row = pl.BlockSpec((TM_FFN, D_MODEL), lambda i: (jnp.minimum(i, n_blocks - 1), 0))
    in_specs = [row, _const_spec((ns, D_MODEL)), _const_spec((1, D_MODEL)),
                _const_spec((D_MODEL, D_FF)), _const_spec((D_MODEL, D_FF)),
                _const_spec((D_FF, D_MODEL))]
    args = [x, xs, g, w1, w3, w2]
    if final:
        in_specs.append(_const_spec((1, D_MODEL)))
        args.append(gf)
    out_specs = [row, pl.BlockSpec((ns, D_MODEL), lambda i: (0, 0))]
    out_shape = [jax.ShapeDtypeStruct((n, D_MODEL), F32), jax.ShapeDtypeStruct((ns, D_MODEL), F32)]
    cast_specs, cast_blocks = [], []
    for w in cast:
        rb = _cast_rows(w.shape[0], n_blocks + 1)
        blocks = w.shape[0] // rb
        cast_specs.append(pl.BlockSpec((rb, w.shape[1]), lambda i, last=blocks - 1: (jnp.minimum(i, last), 0)))
        cast_blocks.append(blocks)
        out_shape.append(jax.ShapeDtypeStruct(w.shape, BF16))
    outs = pl.pallas_call(
        functools.partial(_ffn_body, final=final, n_blocks=n_blocks, cast_blocks=tuple(cast_blocks)),
        grid=(n_blocks + 1,),
        in_specs=in_specs + cast_specs,
        out_specs=out_specs + cast_specs,
        out_shape=out_shape,
        compiler_params=_params(),
        name="ffn_final" if final else "ffn",
    )(*args, *cast)
    return outs[0], outs[1], outs[2:]


def _project(h_ref, n, win_ref, gmn_ref, u_ref, v_ref, q_ref, kv_ref, ga_ref, gb_ref):
    def proj(lo, hi):
        return _dot(h_ref[:n], win_ref[:, lo:hi])

    u_ref[:n] = jax.nn.gelu(proj(C_U, C_V)).astype(u_ref.dtype)
    v_ref[:n] = _rms(jax.nn.gelu(proj(C_V, C_Q)), gmn_ref[...]).astype(v_ref.dtype)
    ga_ref[:n] = jax.nn.sigmoid(proj(C_GA, C_GB)).astype(ga_ref.dtype)
    gb_ref[:n] = jax.nn.sigmoid(proj(C_GB, IN_COLS)).astype(gb_ref.dtype)
    q_ref[:n] = (proj(C_Q, C_KV) * (SCALE * LOG2E)).astype(q_ref.dtype)
    kv_ref[:n] = proj(C_KV, C_GA)


def _inproj_body(x_ref, g_ref, w_ref, gm_ref, u_ref, v_ref, q_ref, kv_ref, ga_ref, gb_ref, h_ref):
    h_ref[...] = _rms(x_ref[...], g_ref[...]).astype(BF16)
    _project(h_ref, TM_MIX, w_ref, gm_ref, u_ref, v_ref, q_ref, kv_ref, ga_ref, gb_ref)


def _inproj(x, g, w_in, gm):
    n = x.shape[0]
    tm = TM_MIX
    row = lambda w: pl.BlockSpec((tm, w), lambda i: (i, 0))
    wide = jax.ShapeDtypeStruct((n, D_MODEL), BF16)
    return pl.pallas_call(
        _inproj_body,
        grid=(n // tm,),
        in_specs=[row(D_MODEL), _const_spec((1, D_MODEL)), _const_spec((D_MODEL, IN_COLS)),
                  _const_spec((1, GM_WIDTH))],
        out_specs=[row(GM_WIDTH), row(GM_WIDTH), row(QW), row(2 * KVW), row(D_MODEL), row(D_MODEL)],
        out_shape=[wide, wide, wide, jax.ShapeDtypeStruct((n, 2 * KVW), F32), wide, wide],
        scratch_shapes=[pltpu.VMEM((tm, D_MODEL), BF16)],
        compiler_params=_params(),
        name="inproj",
    )(x, g, w_in, gm)


def _stage_padded(src, dst_ref):
    rows = src.shape[0]
    low = lax.broadcasted_iota(jnp.int32, src.shape, 1) < HEAD_DIM
    rolled = pltpu.roll(src, HEAD_DIM, 1)
    dst_ref[0, :rows] = jnp.where(low, src, 0.0).astype(BF16)
    dst_ref[1, :rows] = jnp.where(low, 0.0, rolled).astype(BF16)
    dst_ref[2, :rows] = jnp.where(low, rolled, 0.0).astype(BF16)
    dst_ref[3, :rows] = jnp.where(low, 0.0, src).astype(BF16)


def _stage_kv(kv, kst_ref, vst_ref):
    rows = kv.shape[0]
    _stage_padded(kv[:, :KVW], kst_ref)
    _stage_padded(kv[:, KVW:], vst_ref)
    pad = kst_ref.shape[1] - rows
    kst_ref[:, rows:, :] = jnp.zeros((N_GROUPS, pad, LANES), BF16)


def _stage_sink_fill(sinks_ref, fill_ref, rows_per_slab, n_keys):
    shape = fill_ref.shape[1:]
    row = lax.broadcasted_iota(jnp.int32, shape, 0)
    lane = lax.broadcasted_iota(jnp.int32, shape, 1)
    for kh in range(N_KV_HEADS):
        for j in range(SLAB_HEADS):
            head = lambda p: kh * Q_PER_KV + SLAB_HEADS * p + j
            snk = jnp.full(shape, sinks_ref[head(KV_SLABS - 1)] * LOG2E, F32)
            for p in range(KV_SLABS - 2, -1, -1):
                snk = jnp.where(row < (p + 1) * rows_per_slab, sinks_ref[head(p)] * LOG2E, snk)
            fill_ref[kh * SLAB_HEADS + j] = jnp.where(lane == n_keys - LANES, snk, NEG)


def _attend(groups, n_keys):
    scores = [_dot_nt(g[0], g[1]) for g in groups]
    probs, rdens = [], []
    for s, (_, _, _, fill, mask_lo, mask_hi) in zip(scores, groups):
        s_lo, s_hi = s[:, :LANES], s[:, LANES:]
        if mask_lo is not None:
            s_lo = jnp.where(mask_lo, s_lo, NEG)
        s_hi = jnp.where(mask_hi, s_hi, fill)
        m = jnp.max(jnp.maximum(s_lo, s_hi), axis=1, keepdims=True)
        e_lo = jnp.exp2(s_lo - m)
        e_hi = jnp.exp2(s_hi - m)
        rdens.append(1.0 / jnp.sum(e_lo + e_hi, axis=1, keepdims=True))
        probs.append(jnp.concatenate([e_lo, e_hi], axis=1).astype(BF16))
    return [_dot(p[:, :n_keys], g[2]) * r for p, r, g in zip(probs, rdens, groups)]


def _slab(kh, p):
    return slice((kh * KV_SLABS + p) * LANES, (kh * KV_SLABS + p + 1) * LANES)


def _attend_rows(q_ref, ob_ref, rows, t, kst_ref, vst_ref, fill_ref, key0, n_keys, mask_lo, mask_hi):
    groups = []
    for kh in range(N_KV_HEADS):
        lhs = jnp.concatenate([q_ref[rows, _slab(kh, p)] for p in range(KV_SLABS)], axis=0)
        for j in range(SLAB_HEADS):
            g = kh * SLAB_HEADS + j
            groups.append((lhs, kst_ref[g, key0:key0 + KEY_TILE, :], vst_ref[g, key0:key0 + n_keys, :],
                           fill_ref[g], mask_lo, mask_hi))
    outs = _attend(groups, n_keys)
    for kh in range(N_KV_HEADS):
        o = outs[kh * SLAB_HEADS]
        for j in range(1, SLAB_HEADS):
            o = o + outs[kh * SLAB_HEADS + j]
        for p in range(KV_SLABS):
            ob_ref[rows, _slab(kh, p)] = o[p * t:(p + 1) * t].astype(BF16)


def _merge(x, oa, ob, ga, gb, wpa_ref, wpb_ref, wout_ref):
    ya = _dot(oa, wpa_ref[...])
    yb = _dot(ob, wpb_ref[...])
    m = (ga.astype(F32) * ya + gb.astype(F32) * yb).astype(BF16)
    return x + _dot(m, wout_ref[...])


def _mix_prompt_body(sinks_ref, x_ref, u_ref, v_ref, q_ref, kvp_ref, kvc_ref, ga_ref, gb_ref,
                     ws_ref, bst_ref, wpa_ref, wpb_ref, wout_ref, o_ref,
                     kst_ref, vt_ref, oa_ref, ob_ref, *, blocks_per_stream):
    tm = TM_MIX
    ti = lax.rem(pl.program_id(0), blocks_per_stream)

    n_gm = tm // GM_CHUNK
    ri = lax.broadcasted_iota(jnp.int32, (GM_CHUNK, GM_CHUNK), 0)
    ci = lax.broadcasted_iota(jnp.int32, (GM_CHUNK, GM_CHUNK), 1)
    gm_mask = _chunk_causal(ri, ci).astype(F32)
    for g in range(GM_GROUPS):
        cols = slice(g * GM_GROUP_DIM, (g + 1) * GM_GROUP_DIM)
        w = (ws_ref[g] * gm_mask).astype(BF16)
        rhs = jnp.concatenate(
            [v_ref[c * GM_CHUNK:(c + 1) * GM_CHUNK, cols] for c in range(n_gm)], axis=1)
        sg = _dot(w, rhs) + bst_ref[:, g:g + 1]
        for c in range(n_gm):
            rows = slice(c * GM_CHUNK, (c + 1) * GM_CHUNK)
            oa_ref[rows, cols] = (u_ref[rows, cols].astype(F32)
                                  * sg[:, c * GM_CHUNK:(c + 1) * GM_CHUNK]).astype(BF16)

    kv = jnp.concatenate([kvp_ref[...], kvc_ref[...]], axis=0)
    _stage_padded(kv[:, :KVW], kst_ref)
    v2 = kv[:, KVW:]
    ones_row = (lax.broadcasted_iota(jnp.int32, (VT_ROWS - HEAD_DIM, tm + WINDOW), 0) == 0).astype(BF16)
    v2s = jnp.concatenate([v2[CHUNK:], jnp.zeros((CHUNK, KVW), F32)], axis=0)
    for shifted, vt in enumerate((v2.T.astype(BF16), v2s.T.astype(BF16))):
        for kh in range(N_KV_HEADS):
            vt_ref[shifted, kh, :HEAD_DIM] = vt[kh * HEAD_DIM:(kh + 1) * HEAD_DIM]
            vt_ref[shifted, kh, HEAD_DIM:] = ones_row
    n_q = KV_SLABS * CHUNK
    qlane = lax.broadcasted_iota(jnp.int32, (1, n_q), 1)
    sink_rows = []
    for kh in range(N_KV_HEADS):
        for j in range(SLAB_HEADS):
            head = lambda p: kh * Q_PER_KV + SLAB_HEADS * p + j
            snk = jnp.full((1, n_q), sinks_ref[head(KV_SLABS - 1)] * LOG2E, F32)
            for p in range(KV_SLABS - 2, -1, -1):
                snk = jnp.where(qlane < (p + 1) * CHUNK, sinks_ref[head(p)] * LOG2E, snk)
            sink_rows.append(snk)
    units = [(c, kh) for c in range(tm // CHUNK) for kh in range(N_KV_HEADS)]
    for u0 in range(0, len(units), ATTN_BATCH):
        batch = units[u0:u0 + ATTN_BATCH]
        scores = []
        for c, kh in batch:
            r0 = c * CHUNK
            qs = jnp.concatenate([q_ref[r0:r0 + CHUNK, _slab(kh, p)] for p in range(KV_SLABS)], axis=0)
            keys = jnp.concatenate([kst_ref[kh * SLAB_HEADS + j, r0:r0 + BAND, :]
                                    for j in range(SLAB_HEADS)], axis=0)
            scores.append(_dot_nt(keys, qs))
        probs = []
        for (c, kh), st in zip(batch, scores):
            r0 = c * CHUNK
            for j in range(SLAB_HEADS):
                s = st[j * BAND:(j + 1) * BAND]
                if r0 < WINDOW:
                    key_ok = ti * tm + r0 - WINDOW + lax.broadcasted_iota(jnp.int32, (BAND, n_q), 0) >= 0
                    s = jnp.where(key_ok, s, NEG)
                snk = sink_rows[kh * SLAB_HEADS + j]
                m = jnp.maximum(jnp.max(s, axis=0, keepdims=True), snk)
                probs.append((jnp.exp2(s - m).astype(BF16), jnp.exp2(snk - m)))
        for n, (c, kh) in enumerate(batch):
            r0 = c * CHUNK
            shifted = (r0 % LANES) // CHUNK
            v0 = r0 - shifted * CHUNK
            vt = vt_ref[shifted, kh, :, v0:v0 + BAND]
            outs = []
            for p, e_sink in probs[n * SLAB_HEADS:(n + 1) * SLAB_HEADS]:
                pv = _dot(vt, p)
                den = pv[HEAD_DIM:HEAD_DIM + 1] + e_sink
                outs.append(pv[:HEAD_DIM] * (1.0 / den))
            o = jnp.concatenate(outs, axis=0).T
            for p in range(KV_SLABS):
                ob_ref[r0:r0 + CHUNK, _slab(kh, p)] = o[p * CHUNK:(p + 1) * CHUNK].astype(BF16)

    o_ref[...] = _merge(x_ref[...], oa_ref[...], ob_ref[...], ga_ref[...], gb_ref[...],
                        wpa_ref, wpb_ref, wout_ref)


def _mix_prompt(sinks, x, u, v, q, kv, ga, gb, ws, bst, wpa, wpb, wout, blocks_per_stream):
    n = x.shape[0]
    tm = TM_MIX
    blk = lambda w: pl.BlockSpec((tm, w), lambda i, s: (i, 0))
    prev = pl.BlockSpec((WINDOW, 2 * KVW), lambda i, s: (jnp.maximum(i * (tm // WINDOW) - 1, 0), 0))
    grid_spec = pltpu.PrefetchScalarGridSpec(
        num_scalar_prefetch=1,
        grid=(n // tm,),
        in_specs=[blk(D_MODEL), blk(GM_WIDTH), blk(GM_WIDTH), blk(QW), prev, blk(2 * KVW),
                  blk(D_MODEL), blk(D_MODEL),
                  _const_spec((GM_GROUPS, GM_CHUNK, GM_CHUNK)), _const_spec((GM_CHUNK, GM_GROUPS)),
                  _const_spec((GM_WIDTH, D_MODEL)), _const_spec((QW, D_MODEL)),
                  _const_spec((D_MODEL, D_MODEL))],
        out_specs=blk(D_MODEL),
        scratch_shapes=[pltpu.VMEM((N_GROUPS, tm + WINDOW, LANES), BF16),
                        pltpu.VMEM((2, N_KV_HEADS, VT_ROWS, tm + WINDOW), BF16),
                        pltpu.VMEM((tm, GM_WIDTH), BF16),
                        pltpu.VMEM((tm, QW), BF16)],
    )
    return pl.pallas_call(
        functools.partial(_mix_prompt_body, blocks_per_stream=blocks_per_stream),
        grid_spec=grid_spec,
        out_shape=jax.ShapeDtypeStruct((n, D_MODEL), F32),
        compiler_params=_params(),
        name="mix_prompt",
    )(sinks, x, u, v, q, kv, kv, ga, gb, ws, bst, wpa, wpb, wout)


def _mix_sample_body(sinks_ref, xs_ref, ck_ref, cv_ref, nmix_ref, win_ref, gmn_ref, wst_ref, bsts_ref,
                     wpa_ref, wpb_ref, wout_ref, os_ref, kvs_ref, vns_ref,
                     h_ref, u_ref, q_ref, ga_ref, gb_ref,
                     kst_ref, vst_ref, fill_ref, oa_ref, ob_ref, *, nb, s, wc):
    n = nb * s
    x = xs_ref[...]
    h_ref[...] = _rms(x, nmix_ref[...]).astype(BF16)
    _project(h_ref, n, win_ref, gmn_ref, u_ref, vns_ref, q_ref, kvs_ref, ga_ref, gb_ref)

    ri = lax.broadcasted_iota(jnp.int32, (n, n), 0)
    ci = lax.broadcasted_iota(jnp.int32, (n, n), 1)
    gm_mask = ((_div(ri, s) == _div(ci, s)) & _chunk_causal(_mod(ri, s), _mod(ci, s))).astype(F32)
    for g in range(GM_GROUPS):
        cols = slice(g * GM_GROUP_DIM, (g + 1) * GM_GROUP_DIM)
        w = (wst_ref[g] * gm_mask).astype(BF16)
        sg = _dot(w, vns_ref[:, cols].astype(BF16)) + bsts_ref[:, g:g + 1]
        oa_ref[:, cols] = (u_ref[:, cols].astype(F32) * sg).astype(BF16)

    n_keys = wc + s
    _stage_sink_fill(sinks_ref, fill_ref, s, n_keys)
    rows = KV_SLABS * s
    q_pos = PAST_LEN + _mod(lax.broadcasted_iota(jnp.int32, (rows, KEY_TILE), 0), s)
    col = lax.broadcasted_iota(jnp.int32, (rows, KEY_TILE), 1)
    dist = _div(q_pos, CHUNK) - _div(PAST_LEN - wc + col, CHUNK)
    valid = (dist >= 0) & (dist <= WINDOW_CHUNKS) & (col < n_keys)
    for b in range(nb):
        r = slice(b * s, (b + 1) * s)
        new = kvs_ref[r, :]
        kv = jnp.concatenate(
            [jnp.concatenate([ck_ref[b], new[:, :KVW]], axis=0),
             jnp.concatenate([cv_ref[b], new[:, KVW:]], axis=0)], axis=1)
        _stage_kv(kv, kst_ref, vst_ref)
        _attend_rows(q_ref, ob_ref, r, s, kst_ref, vst_ref, fill_ref, 0, n_keys,
                     valid[:, :LANES], valid[:, LANES:])

    os_ref[...] = _merge(x, oa_ref[...], ob_ref[...], ga_ref[...], gb_ref[...],
                         wpa_ref, wpb_ref, wout_ref)


def _mix_sample(sinks, xs, ck, cv, nmix, win, gmn, wst, bsts, wpa, wpb, wout, nb, s):
    ns = xs.shape[0]
    wc = ck.shape[1]
    full = lambda shape: pl.BlockSpec(shape, lambda i, sk: (0,) * len(shape))
    args = (xs, ck, cv, nmix, win, gmn, wst, bsts, wpa, wpb, wout)
    wide = pltpu.VMEM((ns, D_MODEL), BF16)
    grid_spec = pltpu.PrefetchScalarGridSpec(
        num_scalar_prefetch=1,
        grid=(1,),
        in_specs=[full(a.shape) for a in args],
        out_specs=[full((ns, D_MODEL)), full((ns, 2 * KVW)), full((ns, GM_WIDTH))],
        scratch_shapes=[wide, wide, wide, wide, wide,
                        pltpu.VMEM((N_GROUPS, KEY_TILE, LANES), BF16),
                        pltpu.VMEM((N_GROUPS, wc + s, LANES), BF16),
                        pltpu.VMEM((N_GROUPS, KV_SLABS * s, LANES), F32),
                        wide, wide],
    )
    return pl.pallas_call(
        functools.partial(_mix_sample_body, nb=nb, s=s, wc=wc),
        grid_spec=grid_spec,
        out_shape=[jax.ShapeDtypeStruct((ns, D_MODEL), F32),
                   jax.ShapeDtypeStruct((ns, 2 * KVW), F32),
                   jax.ShapeDtypeStruct((ns, GM_WIDTH), F32)],
        compiler_params=_params(),
        name="mix_sample",
    )(sinks, *args)


def kernel(x_prompt, x_sample, cache_k, cache_v, norm_ffn1, ffn1_w1, ffn1_w3, ffn1_w2, norm_mix, w_in, gm_norm, gm_ws, gm_bs, sinks, w_pa, w_pb, w_out, norm_ffn2, ffn2_w1, ffn2_w3, ffn2_w2, norm_final):
    depth = norm_ffn1.shape[0]
    b, t, _ = x_prompt.shape
    nb, s, _ = x_sample.shape
    wc = cache_k.shape[2]
    assert t % TM_MIX == 0 and (b * t) % TM_FFN == 0 and t >= WINDOW and TM_MIX % WINDOW == 0
    assert wc + s < KEY_TILE and wc % 16 == 0 and s % 16 == 0
    row = lambda a: a.reshape(1, -1).astype(F32)
    gf = row(norm_final)

    xp = x_prompt.reshape(b * t, D_MODEL)
    xs = x_sample.reshape(nb * s, D_MODEL)
    kp_l, vp_l, ks_l, vs_l, gs_l = [], [], [], [], []
    for l in range(depth):
        last = l == depth - 1
        ws = gm_ws[l].astype(F32)
        bs = gm_bs[l].astype(F32)
        snk = sinks[l].astype(F32)
        nmix, gmn = row(norm_mix[l]), row(gm_norm[l])

        later = tuple(w[l].astype(F32) for w in (w_in, w_pa, w_pb, w_out, ffn2_w1, ffn2_w3, ffn2_w2))
        xp, xs, (win, wpa, wpb, wout, f2w1, f2w3, f2w2) = _ffn(
            xp, xs, row(norm_ffn1[l]), ffn1_w1[l].astype(BF16), ffn1_w3[l].astype(BF16),
            ffn1_w2[l].astype(BF16), None, later)
        u, v, q, kv, ga, gb = _inproj(xp, nmix, win, gmn)
        xp = _mix_prompt(snk, xp, u, v, q, kv, ga, gb, ws, bs.T, wpa, wpb, wout, t // TM_MIX)
        xs, kv_s, vn_s = _mix_sample(
            snk, xs, cache_k[l].reshape(nb, wc, KVW), cache_v[l].reshape(nb, wc, KVW), nmix, win, gmn,
            jnp.tile(ws[:, :s, :s], (1, nb, nb)), jnp.tile(bs[:, :s].T, (nb, 1)), wpa, wpb, wout, nb, s)
        xp, xs, _ = _ffn(xp, xs, row(norm_ffn2[l]), f2w1, f2w3, f2w2, gf if last else None)

        keep = min(WINDOW, t)
        kv_tail = kv.reshape(b, t, 2 * KVW)[:, t - keep:]
        kp_l.append(kv_tail[:, :, :KVW].reshape(b, keep, N_KV_HEADS, HEAD_DIM))
        vp_l.append(kv_tail[:, :, KVW:].reshape(b, keep, N_KV_HEADS, HEAD_DIM))
        ks_l.append(kv_s[:, :KVW].reshape(nb, s, N_KV_HEADS, HEAD_DIM))
        vs_l.append(kv_s[:, KVW:].reshape(nb, s, N_KV_HEADS, HEAD_DIM))
        gs_l.append(vn_s.reshape(nb, s, GM_WIDTH))

    return (xp.reshape(b, t, D_MODEL), xs.reshape(nb, s, D_MODEL), jnp.stack(kp_l), jnp.stack(vp_l),
            jnp.stack(ks_l), jnp.stack(vs_l), jnp.stack(gs_l))
```
